```python
import math
import jax, jax.numpy as jnp
from jax import lax
import numpy as np

D_MODEL = 1024
BATCH = 16
SEQ = 2048
DEPTH = 1

CHUNK = 64
Q_BLOCK = 128
ROPE_THETA = 500000.0
D_MIX = D_MODEL
MLA_HEADS = 8
MLA_NOPE_DIM = 64
MLA_ROPE_DIM = 32
MLA_V_DIM = 64
MLA_QK_DIM = MLA_NOPE_DIM + MLA_ROPE_DIM
Q_RANK = 384
KV_RANK = 256
DIFF_HEADS = 4
DIFF_HEAD_DIM = 64
DIFF_ROPE_DIM = DIFF_HEAD_DIM // 4
IN_WIDTHS = (Q_RANK, KV_RANK, MLA_ROPE_DIM,
             DIFF_HEADS * 2 * DIFF_HEAD_DIM, DIFF_HEADS * 2 * DIFF_HEAD_DIM, DIFF_HEADS * 2 * DIFF_HEAD_DIM)
IN_COLS = sum(IN_WIDTHS)
N_EXPERTS = 32
TOP_K = 4
EXPERT_DFF = D_MODEL
SWIGLU_LIMIT = 7.0
SWIGLU_ALPHA = 1.702
MOE_BLOCK = 128
N_MOD = 6

kernel_name = "hybrid_mla_diffattn_moe_adaln_block"


def rms_norm(x, g, eps=1e-6):
    xf = x.astype(jnp.float32)
    y = xf * lax.rsqrt(jnp.mean(xf * xf, axis=-1, keepdims=True) + eps)
    return (y * g.astype(jnp.float32)).astype(x.dtype)


def apply_rope(x, positions, rot_dim):
    half = rot_dim // 2
    inv_freq = ROPE_THETA ** (-jnp.arange(half, dtype=jnp.float32) / half)
    ang = positions.astype(jnp.float32)[:, :, None] * inv_freq
    ang = ang.reshape(ang.shape[:2] + (1,) * (x.ndim - 3) + (half,))
    cos, sin = jnp.cos(ang), jnp.sin(ang)
    xr = x[..., :rot_dim].astype(jnp.float32)
    x1, x2 = xr[..., :half], xr[..., half:]
    rot = jnp.concatenate([x1 * cos - x2 * sin, x2 * cos + x1 * sin], axis=-1).astype(x.dtype)
    return jnp.concatenate([rot, x[..., rot_dim:]], axis=-1)


def _split_query_blocks(q):
    *lead, s, d = q.shape
    return jnp.moveaxis(q.reshape(*lead, s // Q_BLOCK, Q_BLOCK, d), -3, 0)


def _merge_query_blocks(o):
    o = jnp.moveaxis(o, 0, -3)
    *lead, nb, qb, d = o.shape
    return o.reshape(*lead, nb * qb, d)


def _chunk_causal_mask(blk, s):
    q_idx = blk * Q_BLOCK + jnp.arange(Q_BLOCK)
    k_idx = jnp.arange(s)
    return (k_idx[None, :] // CHUNK) <= (q_idx[:, None] // CHUNK)


def mla_attention(q, k, v):
    s = q.shape[-2]
    scale = MLA_QK_DIM ** -0.5

    def one_block(args):
        qb, blk = args
        sc = jnp.einsum('bhqd,bhkd->bhqk', qb, k).astype(jnp.float32) * scale
        sc = jnp.where(_chunk_causal_mask(blk, s), sc, -jnp.inf)
        p = jax.nn.softmax(sc, axis=-1).astype(v.dtype)
        return jnp.einsum('bhqk,bhkd->bhqd', p, v)

    out = lax.map(one_block, (_split_query_blocks(q), jnp.arange(s // Q_BLOCK)))
    return _merge_query_blocks(out)


def diff_attention(q, k, v, lam):
    s = q.shape[-2]
    scale = DIFF_HEAD_DIM ** -0.5

    def one_block(args):
        qb, blk = args
        sc = jnp.einsum('bhcqd,bhckd->bhcqk', qb, k).astype(jnp.float32) * scale
        sc = jnp.where(_chunk_causal_mask(blk, s), sc, -jnp.inf)
        p = jax.nn.softmax(sc, axis=-1)
        w = (p[:, :, 0] - lam * p[:, :, 1]).astype(v.dtype)
        return jnp.einsum('bhqk,bhkd->bhqd', w, v)

    out = lax.map(one_block, (_split_query_blocks(q), jnp.arange(s // Q_BLOCK)))
    return _merge_query_blocks(out)


def hybrid_mixer(h, positions, w_in, g_q_norm, w_uq, g_kv_norm, w_ukv,
                 lambda_q1, lambda_k1, lambda_q2, lambda_k2, g_subln, w_out, lambda_init):
    b, s, _ = h.shape
    proj = h @ w_in
    cuts = list(np.cumsum(IN_WIDTHS)[:-1])
    c_q, c_kv, k_pe, dq, dk, dv = jnp.split(proj, cuts, axis=-1)

    q = (rms_norm(c_q, g_q_norm) @ w_uq).reshape(b, s, MLA_HEADS, MLA_QK_DIM)
    q = jnp.concatenate([q[..., :MLA_NOPE_DIM],
                         apply_rope(q[..., MLA_NOPE_DIM:], positions, MLA_ROPE_DIM)], axis=-1)
    kv = (rms_norm(c_kv, g_kv_norm) @ w_ukv).reshape(b, s, MLA_HEADS, MLA_NOPE_DIM + MLA_V_DIM)
    k_nope, v_m = kv[..., :MLA_NOPE_DIM], kv[..., MLA_NOPE_DIM:]
    k_pe = apply_rope(k_pe[:, :, None, :], positions, MLA_ROPE_DIM)
    k = jnp.concatenate([k_nope, jnp.broadcast_to(k_pe, (b, s, MLA_HEADS, MLA_ROPE_DIM))], axis=-1)
    o_mla = mla_attention(q.transpose(0, 2, 1, 3), k.transpose(0, 2, 1, 3), v_m.transpose(0, 2, 1, 3))
    o_mla = o_mla.transpose(0, 2, 1, 3).reshape(b, s, MLA_HEADS * MLA_V_DIM)

    dq = apply_rope(dq.reshape(b, s, DIFF_HEADS, 2, DIFF_HEAD_DIM), positions, DIFF_ROPE_DIM)
    dk = apply_rope(dk.reshape(b, s, DIFF_HEADS, 2, DIFF_HEAD_DIM), positions, DIFF_ROPE_DIM)
    dv = dv.reshape(b, s, DIFF_HEADS, 2 * DIFF_HEAD_DIM).transpose(0, 2, 1, 3)
    lam = (jnp.exp(jnp.sum(lambda_q1.astype(jnp.float32) * lambda_k1.astype(jnp.float32)))
           - jnp.exp(jnp.sum(lambda_q2.astype(jnp.float32) * lambda_k2.astype(jnp.float32)))
           + lambda_init)
    o_diff = diff_attention(dq.transpose(0, 2, 3, 1, 4), dk.transpose(0, 2, 3, 1, 4), dv, lam)
    o_diff = rms_norm(o_diff, g_subln, eps=1e-5) * (1.0 - lambda_init)
    o_diff = o_diff.transpose(0, 2, 1, 3).reshape(b, s, DIFF_HEADS * 2 * DIFF_HEAD_DIM)

    return jnp.concatenate([o_mla, o_diff], axis=-1) @ w_out


def clamped_swiglu(gu):
    gate, up = gu[..., ::2], gu[..., 1::2]
    gate = jnp.minimum(gate, SWIGLU_LIMIT)
    up = jnp.clip(up, -SWIGLU_LIMIT, SWIGLU_LIMIT)
    return gate * jax.nn.sigmoid(SWIGLU_ALPHA * gate) * (up + 1.0)


def moe_ffn(t, w_router, b_router, w_gate_up, b_gate_up, w_down, b_down):
    n, d = t.shape
    logits = (t @ w_router + b_router).astype(jnp.float32)
    top_val, top_idx = lax.top_k(logits, TOP_K)
    top_w = jax.nn.softmax(top_val, axis=-1)
    a = n * TOP_K
    flat_e = top_idx.reshape(a)
    flat_tok = jnp.arange(a, dtype=jnp.int32) // TOP_K
    flat_w = top_w.reshape(a)
    order = jnp.argsort(flat_e)
    sorted_e = flat_e[order]
    counts = jnp.bincount(flat_e, length=N_EXPERTS)
    padded = (counts + MOE_BLOCK - 1) // MOE_BLOCK * MOE_BLOCK
    start_sorted = jnp.cumsum(counts) - counts
    end_padded = jnp.cumsum(padded)
    start_padded = end_padded - padded
    slot = start_padded[sorted_e] + jnp.arange(a) - start_sorted[sorted_e]
    p = a + N_EXPERTS * MOE_BLOCK
    slot_tok = jnp.zeros((p,), jnp.int32).at[slot].set(flat_tok[order])
    slot_w = jnp.zeros((p,), jnp.float32).at[slot].set(flat_w[order])
    nblk = p // MOE_BLOCK
    blk_expert = jnp.minimum(
        jnp.searchsorted(end_padded, jnp.arange(nblk) * MOE_BLOCK, side='right'), N_EXPERTS - 1)

    def one_block(args):
        tok, wgt, e = args
        xb = t[tok]
        act = clamped_swiglu(xb @ w_gate_up[e] + b_gate_up[e])
        y = act @ w_down[e] + b_down[e]
        return y * wgt[:, None].astype(y.dtype)

    y = lax.map(one_block, (slot_tok.reshape(nblk, MOE_BLOCK),
                            slot_w.reshape(nblk, MOE_BLOCK), blk_expert))
    return jnp.zeros_like(t).at[slot_tok].add(y.reshape(p, d))


def setup_inputs(seed: int = 0) -> dict:
    key = jax.random.key(seed)
    ks = jax.random.split(key, 32)
    f32 = jnp.float32
    nrm = lambda k, shape, scale: jax.random.normal(k, shape, f32) * scale
    gain = lambda k, shape: 1.0 + 0.02 * jax.random.normal(k, shape, f32)
    L = DEPTH
    offset = jax.random.randint(ks[2], (BATCH, 1), 0, 4096, dtype=jnp.int32)
    positions = offset + jnp.arange(SEQ, dtype=jnp.int32)[None, :]
    return {
        "x": nrm(ks[0], (BATCH, SEQ, D_MODEL), 1.0),
        "c": nrm(ks[1], (BATCH, D_MODEL), 1.0),
        "positions": positions,
        "w_ada": nrm(ks[3], (L, D_MODEL, N_MOD * D_MODEL), 0.5 * D_MODEL ** -0.5),
        "b_ada": nrm(ks[4], (L, N_MOD * D_MODEL), 0.01),
        "g_norm1": gain(ks[5], (L, D_MODEL)),
        "w_in": nrm(ks[6], (L, D_MODEL, IN_COLS), D_MODEL ** -0.5),
        "g_q_norm": gain(ks[7], (L, Q_RANK)),
        "w_uq": nrm(ks[8], (L, Q_RANK, MLA_HEADS * MLA_QK_DIM), Q_RANK ** -0.5),
        "g_kv_norm": gain(ks[9], (L, KV_RANK)),
        "w_ukv": nrm(ks[10], (L, KV_RANK, MLA_HEADS * (MLA_NOPE_DIM + MLA_V_DIM)), KV_RANK ** -0.5),
        "lambda_q1": nrm(ks[11], (L, DIFF_HEAD_DIM), 0.1),
        "lambda_k1": nrm(ks[12], (L, DIFF_HEAD_DIM), 0.1),
        "lambda_q2": nrm(ks[13], (L, DIFF_HEAD_DIM), 0.1),
        "lambda_k2": nrm(ks[14], (L, DIFF_HEAD_DIM), 0.1),
        "g_subln": gain(ks[15], (L, 2 * DIFF_HEAD_DIM)),
        "w_out": nrm(ks[16], (L, D_MIX, D_MODEL), D_MIX ** -0.5),
        "g_norm2": gain(ks[17], (L, D_MODEL)),
        "w_router": nrm(ks[18], (L, D_MODEL, N_EXPERTS), D_MODEL ** -0.5),
        "b_router": nrm(ks[19], (L, N_EXPERTS), 0.01),
        "w_gate_up": nrm(ks[20], (L, N_EXPERTS, D_MODEL, 2 * EXPERT_DFF), D_MODEL ** -0.5),
        "b_gate_up": nrm(ks[21], (L, N_EXPERTS, 2 * EXPERT_DFF), 0.01),
        "w_down": nrm(ks[22], (L, N_EXPERTS, EXPERT_DFF, D_MODEL), EXPERT_DFF ** -0.5),
        "b_down": nrm(ks[23], (L, N_EXPERTS, D_MODEL), 0.01),
        "g_final": gain(ks[24], (D_MODEL,)),
    }


def reference(x, c, positions, w_ada, b_ada, g_norm1, w_in, g_q_norm, w_uq, g_kv_norm, w_ukv,
              lambda_q1, lambda_k1, lambda_q2, lambda_k2, g_subln, w_out, g_norm2,
              w_router, b_router, w_gate_up, b_gate_up, w_down, b_down, g_final):
    b, s, d = x.shape
    for l in range(DEPTH):
        lambda_init = 0.8 - 0.6 * math.exp(-0.3 * l)
        mod = jax.nn.silu(c) @ w_ada[l] + b_ada[l]
        sh1, sc1, gt1, sh2, sc2, gt2 = jnp.split(mod[:, None, :], N_MOD, axis=-1)
        h = rms_norm(x, g_norm1[l]) * (1.0 + sc1) + sh1
        mix = hybrid_mixer(h, positions, w_in[l], g_q_norm[l], w_uq[l], g_kv_norm[l], w_ukv[l],
                           lambda_q1[l], lambda_k1[l], lambda_q2[l], lambda_k2[l], g_subln[l],
                           w_out[l], lambda_init)
        x = x + gt1 * mix
        h = rms_norm(x, g_norm2[l]) * (1.0 + sc2) + sh2
        y = moe_ffn(h.reshape(b * s, d), w_router[l], b_router[l], w_gate_up[l], b_gate_up[l],
                    w_down[l], b_down[l]).reshape(b, s, d)
        x = x + gt2 * y
    return rms_norm(x, g_final)
```

```python
import functools
import math

import jax
import jax.numpy as jnp
from jax import lax
from jax.experimental import pallas as pl
from jax.experimental.pallas import tpu as pltpu

F32 = jnp.float32
BF16 = jnp.bfloat16
HIGHEST = lax.Precision.HIGHEST

D_MODEL = 1024
CHUNK = 64
ROPE_THETA = 500000.0
MLA_HEADS = 8
MLA_NOPE = 64
MLA_ROPE = 32
MLA_V = 64
MLA_QK = MLA_NOPE + MLA_ROPE
Q_RANK = 384
KV_RANK = 256
DIFF_HEADS = 4
DIFF_DIM = 64
DIFF_ROPE = DIFF_DIM // 4
N_EXPERTS = 32
TOP_K = 4
EXPERT_DFF = D_MODEL
SWIGLU_LIMIT = 7.0
SWIGLU_ALPHA = 1.702
N_MOD = 6
LAMBDA_INIT = 0.8 - 0.6 * math.exp(-0.3 * 0)

LANES = 128
HEAD_PAD = LANES
TOKEN_TILE = 512
ATTN_TILE = 256
MOE_ROWS = 256
VMEM_LIMIT = 48 * 1024 * 1024
NEG_BIG = -1e30

MLA_X1 = 96
MLA_HALF = MLA_ROPE // 2
DIFF_HALF = DIFF_ROPE // 2

C_Q = 0
C_KV = Q_RANK
C_KPE = Q_RANK + KV_RANK
C_DQ = C_KPE + LANES
C_DK = C_DQ + DIFF_HEADS * 2 * DIFF_DIM
C_DV = C_DK + DIFF_HEADS * 2 * DIFF_DIM
IN_COLS_PAD = C_DV + DIFF_HEADS * 2 * DIFF_DIM


def _rms(x, g, eps):
    return x * lax.rsqrt(jnp.mean(x * x, axis=-1, keepdims=True) + eps) * g


def _params(sem):
    return pltpu.CompilerParams(dimension_semantics=sem, vmem_limit_bytes=VMEM_LIMIT)


def _ada_kernel(c_ref, w_ref, b_ref, o_ref):
    c = c_ref[...]
    sc = c / (1.0 + jnp.exp(-c))
    o_ref[...] = jnp.dot(sc, w_ref[...], preferred_element_type=F32, precision=HIGHEST) + b_ref[...]


def _ada_mod(c, w_ada, b_ada):
    b, d = c.shape
    n = w_ada.shape[1]
    return pl.pallas_call(
        _ada_kernel,
        grid=(n // d,),
        in_specs=[pl.BlockSpec((b, d), lambda j: (0, 0)),
                  pl.BlockSpec((d, d), lambda j: (0, j)),
                  pl.BlockSpec((1, d), lambda j: (0, j))],
        out_specs=pl.BlockSpec((b, d), lambda j: (0, j)),
        out_shape=jax.ShapeDtypeStruct((b, n), F32),
        compiler_params=_params(("arbitrary",)),
        name="ada_mod",
    )(c, w_ada, b_ada.reshape(1, n))


def _rope(t, cos, sin_a, sin_b, shift):
    return (t * cos + pltpu.roll(t, LANES - shift, 1) * sin_a + pltpu.roll(t, shift, 1) * sin_b)


def _inproj_kernel(x_ref, pos_ref, mod_ref, g1_ref, win_ref, gq_ref, wuq_ref, gkv_ref, wukv_ref,
                   freq_ref, mask_ref, qm_ref, km_ref, vm_ref, dq_ref, dk_ref, dv_ref):
    x = x_ref[0]
    mod = mod_ref[0]
    h = _rms(x, g1_ref[...], 1e-6) * (1.0 + mod[1:2]) + mod[0:1]
    proj = jnp.dot(h.astype(BF16), win_ref[...], preferred_element_type=F32)

    ang = pos_ref[0].astype(F32) * freq_ref[...]
    cos, sin = jnp.cos(ang), jnp.sin(ang)
    mk = mask_ref[...]
    cos_m = jnp.where(mk[0:1] > 0, cos, 1.0)
    sa_m, sb_m = sin * mk[1:2], sin * mk[2:3]
    cos_d = jnp.where(mk[3:4] > 0, cos, 1.0)
    sa_d, sb_d = sin * mk[4:5], sin * mk[5:6]
    mla_scale = MLA_QK ** -0.5
    diff_scale = DIFF_DIM ** -0.5

    cq = _rms(proj[:, C_Q:C_Q + Q_RANK], gq_ref[...], 1e-6)
    q = jnp.dot(cq.astype(BF16), wuq_ref[...], preferred_element_type=F32)
    ckv = _rms(proj[:, C_KV:C_KV + KV_RANK], gkv_ref[...], 1e-6)
    kv = jnp.dot(ckv.astype(BF16), wukv_ref[...], preferred_element_type=F32)
    kpe = _rope(proj[:, C_KPE:C_KPE + LANES], cos_m, sa_m, sb_m, MLA_HALF)
    cq_m, saq_m, sbq_m = cos_m * mla_scale, sa_m * mla_scale, sb_m * mla_scale
    for hd in range(MLA_HEADS):
        sl = slice(hd * HEAD_PAD, (hd + 1) * HEAD_PAD)
        qm_ref[0, :, sl] = _rope(q[:, sl], cq_m, saq_m, sbq_m, MLA_HALF).astype(BF16)
        km_ref[0, :, sl] = (kv[:, sl] + kpe).astype(BF16)
    vm_ref[0] = kv[:, MLA_HEADS * HEAD_PAD:].astype(BF16)

    cq_d, saq_d, sbq_d = cos_d * diff_scale, sa_d * diff_scale, sb_d * diff_scale
    for hd in range(DIFF_HEADS):
        sl = slice(hd * LANES, (hd + 1) * LANES)
        tq = proj[:, C_DQ + hd * LANES:C_DQ + (hd + 1) * LANES]
        tk = proj[:, C_DK + hd * LANES:C_DK + (hd + 1) * LANES]
        dq_ref[0, :, sl] = _rope(tq, cq_d, saq_d, sbq_d, DIFF_HALF).astype(BF16)
        dk_ref[0, :, sl] = _rope(tk, cos_d, sa_d, sb_d, DIFF_HALF).astype(BF16)
    dv_ref[0] = proj[:, C_DV:].astype(BF16)


def _rope_tables():
    lane = jnp.arange(LANES)
    f_m = ROPE_THETA ** (-jnp.arange(MLA_HALF, dtype=F32) / MLA_HALF)
    f_d = ROPE_THETA ** (-jnp.arange(DIFF_HALF, dtype=F32) / DIFF_HALF)
    m_x1 = (lane >= MLA_X1) & (lane < MLA_X1 + MLA_HALF)
    m_x2 = (lane >= MLA_X1 + MLA_HALF) & (lane < MLA_X1 + 2 * MLA_HALF)
    d_x1 = (lane % DIFF_DIM) < DIFF_HALF
    d_x2 = ((lane % DIFF_DIM) >= DIFF_HALF) & ((lane % DIFF_DIM) < 2 * DIFF_HALF)
    freq = jnp.where(m_x1 | m_x2, f_m[(lane - MLA_X1) % MLA_HALF], 0.0)
    freq = jnp.where(d_x1 | d_x2, f_d[lane % DIFF_HALF], freq)
    z = jnp.zeros((LANES,), F32)
    masks = jnp.stack([(m_x1 | m_x2).astype(F32), -m_x1.astype(F32), m_x2.astype(F32),
                       (d_x1 | d_x2).astype(F32), -d_x1.astype(F32), d_x2.astype(F32), z, z])
    return freq.reshape(1, LANES).astype(F32), masks


def _pad_in_weights(w_in, w_uq, w_ukv):
    d = w_in.shape[0]
    c_kpe = Q_RANK + KV_RANK
    kpe = jnp.concatenate([jnp.zeros((d, MLA_X1), F32), w_in[:, c_kpe:c_kpe + MLA_ROPE]], axis=1)
    w_in_p = jnp.concatenate([w_in[:, :c_kpe], kpe, w_in[:, c_kpe + MLA_ROPE:]], axis=1)
    uq = w_uq.reshape(Q_RANK, MLA_HEADS, MLA_QK)
    uq_p = jnp.concatenate([uq[:, :, :MLA_NOPE],
                            jnp.zeros((Q_RANK, MLA_HEADS, MLA_X1 - MLA_NOPE), F32),
                            uq[:, :, MLA_NOPE:]], axis=2).reshape(Q_RANK, MLA_HEADS * HEAD_PAD)
    ukv = w_ukv.reshape(KV_RANK, MLA_HEADS, MLA_NOPE + MLA_V)
    uk_p = jnp.concatenate([ukv[:, :, :MLA_NOPE],
                            jnp.zeros((KV_RANK, MLA_HEADS, HEAD_PAD - MLA_NOPE), F32)],
                           axis=2).reshape(KV_RANK, MLA_HEADS * HEAD_PAD)
    uv = ukv[:, :, MLA_NOPE:].reshape(KV_RANK, MLA_HEADS * MLA_V)
    return w_in_p.astype(BF16), uq_p.astype(BF16), jnp.concatenate([uk_p, uv], axis=1).astype(BF16)


def _in_projection(x, positions, mod, g_norm1, w_in_p, g_q, w_uq_p, g_kv, w_ukv_p):
    b, s, d = x.shape
    tm = min(TOKEN_TILE, s)
    freq, masks = _rope_tables()
    full = lambda a: pl.BlockSpec(a.shape, lambda i, j: (0,) * a.ndim)
    tile = lambda w: pl.BlockSpec((1, tm, w), lambda i, j: (i, j, 0))
    wide, narrow = MLA_HEADS * HEAD_PAD, DIFF_HEADS * 2 * DIFF_DIM
    g1, gq, gkv = g_norm1.reshape(1, d), g_q.reshape(1, Q_RANK), g_kv.reshape(1, KV_RANK)
    out_widths = (wide, wide, MLA_HEADS * MLA_V, narrow, narrow, narrow)
    return pl.pallas_call(
        _inproj_kernel,
        grid=(b, s // tm),
        in_specs=[tile(d), tile(1), pl.BlockSpec((1, N_MOD, d), lambda i, j: (i, 0, 0)),
                  full(g1), full(w_in_p), full(gq), full(w_uq_p), full(gkv), full(w_ukv_p),
                  full(freq), full(masks)],
        out_specs=[tile(w) for w in out_widths],
        out_shape=[jax.ShapeDtypeStruct((b, s, w), BF16) for w in out_widths],
        compiler_params=_params(("parallel", "parallel")),
        name="in_projection",
    )(x, positions.reshape(b, s, 1), mod, g1, w_in_p, gq, w_uq_p, gkv, w_ukv_p, freq, masks)


def _diag_mask(t):
    r = lax.broadcasted_iota(jnp.int32, (t, t), 0) // CHUNK
    c = lax.broadcasted_iota(jnp.int32, (t, t), 1) // CHUNK
    return c <= r


def _softmax_step(s, v, m_ref, l_ref, acc_ref):
    m_old = m_ref[...]
    m_new = jnp.maximum(m_old, jnp.max(s, axis=-1, keepdims=True))
    p = jnp.exp(s - m_new)
    alpha = jnp.exp(m_old - m_new)
    l_ref[...] = alpha * l_ref[...] + jnp.sum(p, axis=-1, keepdims=True)
    acc_ref[...] = alpha * acc_ref[...] + jnp.dot(p.astype(BF16), v, preferred_element_type=F32)
    m_ref[...] = m_new


def _qk(q, k):
    return lax.dot_general(q, k, (((1,), (1,)), ((), ())), preferred_element_type=F32)


def _attn_loop(q_ref, k_ref, v_ref, emit, m_refs, l_refs, acc_refs, seq, t):
    n_maps = len(m_refs)
    width = q_ref.shape[-1] // n_maps

    def q_body(qi, carry):
        q0 = pl.multiple_of(qi * t, t)
        for i in range(n_maps):
            m_refs[i][...] = jnp.full(m_refs[i].shape, NEG_BIG, F32)
            l_refs[i][...] = jnp.zeros(l_refs[i].shape, F32)
            acc_refs[i][...] = jnp.zeros(acc_refs[i].shape, F32)
        qs = [q_ref[0, pl.ds(q0, t), i * width:(i + 1) * width] for i in range(n_maps)]

        def kv_step(k0, mask):
            v = v_ref[0, pl.ds(k0, t), :]
            for i in range(n_maps):
                s = _qk(qs[i], k_ref[0, pl.ds(k0, t), i * width:(i + 1) * width])
                if mask is not None:
                    s = jnp.where(mask, s, NEG_BIG)
                _softmax_step(s, v, m_refs[i], l_refs[i], acc_refs[i])

        def kv_body(j, c):
            kv_step(pl.multiple_of(j * t, t), None)
            return c

        lax.fori_loop(0, qi, kv_body, 0)
        kv_step(q0, _diag_mask(t))
        emit(q0)
        return carry

    lax.fori_loop(0, seq // t, q_body, 0)


def _mla_attn_kernel(q_ref, k_ref, v_ref, o_ref, m0, m1, l0, l1, a0, a1, *, seq, t):
    def emit(q0):
        lane = lax.broadcasted_iota(jnp.int32, (t, LANES), 1)
        o = jnp.where(lane < MLA_V, a0[...] / l0[...], a1[...] / l1[...])
        o_ref[0, pl.ds(q0, t), :] = o.astype(BF16)

    _attn_loop(q_ref, k_ref, v_ref, emit, (m0, m1), (l0, l1), (a0, a1), seq, t)


def _diff_attn_kernel(lam_ref, gs_ref, q_ref, k_ref, v_ref, o_ref, m0, m1, l0, l1, a0, a1, *, seq, t):
    lv = lam_ref[...]
    lam = (jnp.exp(jnp.sum(lv[0:1] * lv[1:2], axis=-1, keepdims=True))
           - jnp.exp(jnp.sum(lv[2:3] * lv[3:4], axis=-1, keepdims=True)) + LAMBDA_INIT)

    def emit(q0):
        o = a0[...] / l0[...] - lam * (a1[...] / l1[...])
        o = _rms(o, gs_ref[...], 1e-5) * (1.0 - LAMBDA_INIT)
        o_ref[0, pl.ds(q0, t), :] = o.astype(BF16)

    _attn_loop(q_ref, k_ref, v_ref, emit, (m0, m1), (l0, l1), (a0, a1), seq, t)


def _attn_scratch(t):
    return ([pltpu.VMEM((t, 1), F32)] * 4) + ([pltpu.VMEM((t, LANES), F32)] * 2)


def _mla_attention(qm, km, vm):
    b, s, _ = qm.shape
    t = min(ATTN_TILE, s)
    pair = 2 * HEAD_PAD
    return pl.pallas_call(
        functools.partial(_mla_attn_kernel, seq=s, t=t),
        grid=(b, MLA_HEADS // 2),
        in_specs=[pl.BlockSpec((1, s, pair), lambda i, j: (i, 0, j)),
                  pl.BlockSpec((1, s, pair), lambda i, j: (i, 0, j)),
                  pl.BlockSpec((1, s, LANES), lambda i, j: (i, 0, j))],
        out_specs=pl.BlockSpec((1, s, LANES), lambda i, j: (i, 0, j)),
        out_shape=jax.ShapeDtypeStruct((b, s, MLA_HEADS * MLA_V), BF16),
        scratch_shapes=_attn_scratch(t),
        compiler_params=_params(("parallel", "parallel")),
        name="mla_attention",
    )(qm, km, vm)


def _diff_attention(lam_vecs, g_subln, dq, dk, dv):
    b, s, _ = dq.shape
    t = min(ATTN_TILE, s)
    blk = pl.BlockSpec((1, s, LANES), lambda i, j: (i, 0, j))
    gs = g_subln.reshape(1, 2 * DIFF_DIM)
    return pl.pallas_call(
        functools.partial(_diff_attn_kernel, seq=s, t=t),
        grid=(b, DIFF_HEADS),
        in_specs=[pl.BlockSpec(lam_vecs.shape, lambda i, j: (0, 0)),
                  pl.BlockSpec(gs.shape, lambda i, j: (0, 0)), blk, blk, blk],
        out_specs=blk,
        out_shape=jax.ShapeDtypeStruct((b, s, DIFF_HEADS * 2 * DIFF_DIM), BF16),
        scratch_shapes=_attn_scratch(t),
        compiler_params=_params(("parallel", "parallel")),
        name="diff_attention",
    )(lam_vecs, gs, dq, dk, dv)


def _outproj_kernel(om_ref, od_ref, x_ref, mod_ref, wout_ref, g2_ref, wr_ref, br_ref,
                    x1_ref, h2_ref, idx_ref, tw_ref, rank_ref, cnt_ref, carry_ref):
    first = (pl.program_id(0) == 0) & (pl.program_id(1) == 0)

    @pl.when(first)
    def _():
        carry_ref[...] = jnp.zeros(carry_ref.shape, F32)

    half = om_ref.shape[-1]
    mix = (jnp.dot(om_ref[0], wout_ref[:half, :], preferred_element_type=F32)
           + jnp.dot(od_ref[0], wout_ref[half:, :], preferred_element_type=F32))
    mod = mod_ref[0]
    x1 = x_ref[0] + mod[2:3] * mix
    h2 = _rms(x1, g2_ref[...], 1e-6) * (1.0 + mod[4:5]) + mod[3:4]
    x1_ref[0] = x1
    h2_ref[0] = h2.astype(BF16)

    logits = lax.dot_general(wr_ref[...], h2, (((1,), (1,)), ((), ())),
                             preferred_element_type=F32, precision=HIGHEST) + br_ref[...]
    n_e, t = logits.shape
    e_iota = lax.broadcasted_iota(jnp.int32, (n_e, t), 0)
    vals, idxs = [], []
    rest = logits
    for _ in range(TOP_K):
        m = jnp.max(rest, axis=0, keepdims=True)
        ik = jnp.min(jnp.where(rest == m, e_iota, n_e), axis=0, keepdims=True)
        vals.append(m)
        idxs.append(ik)
        rest = jnp.where(e_iota == ik, -jnp.inf, rest)
    ex = [jnp.exp(v - vals[0]) for v in vals]
    den = ex[0] + ex[1] + ex[2] + ex[3]
    sel = (e_iota == idxs[0]) | (e_iota == idxs[1]) | (e_iota == idxs[2]) | (e_iota == idxs[3])

    before = (lax.broadcasted_iota(jnp.int32, (t, t), 0)
              < lax.broadcasted_iota(jnp.int32, (t, t), 1)).astype(BF16)
    prefix = jnp.dot(sel.astype(BF16), before, preferred_element_type=F32)
    pos = carry_ref[...] + prefix
    for k in range(TOP_K):
        idx_ref[0, k:k + 1, :] = idxs[k]
        tw_ref[0, k:k + 1, :] = ex[k] / den
        rk = jnp.sum(jnp.where(e_iota == idxs[k], pos, 0.0), axis=0, keepdims=True)
        rank_ref[0, k:k + 1, :] = rk.astype(jnp.int32)
    total = carry_ref[...] + jnp.sum(sel.astype(F32), axis=1, keepdims=True)
    carry_ref[...] = total
    cnt_ref[...] = total.astype(jnp.int32)


def _out_projection(o_mla, o_diff, x, mod, w_out, g_norm2, w_router, b_router):
    b, s, d = x.shape
    tm = min(TOKEN_TILE, s)
    nt = s // tm
    half = o_mla.shape[-1]
    full = lambda a: pl.BlockSpec(a.shape, lambda i, j: (0,) * a.ndim)
    tile = lambda w: pl.BlockSpec((1, tm, w), lambda i, j: (i, j, 0))
    route = pl.BlockSpec((1, TOP_K, tm), lambda i, j: (i * nt + j, 0, 0))
    g2 = g_norm2.reshape(1, d)
    wr = w_router.T
    br = b_router.reshape(N_EXPERTS, 1)
    w_out_b = w_out.astype(BF16)
    return pl.pallas_call(
        _outproj_kernel,
        grid=(b, nt),
        in_specs=[tile(half), tile(half), tile(d), pl.BlockSpec((1, N_MOD, d), lambda i, j: (i, 0, 0)),
                  full(w_out_b), full(g2), full(wr), full(br)],
        out_specs=[tile(d), tile(d), route, route, route,
                   pl.BlockSpec((N_EXPERTS, 1), lambda i, j: (0, 0))],
        out_shape=[jax.ShapeDtypeStruct((b, s, d), F32), jax.ShapeDtypeStruct((b, s, d), BF16),
                   jax.ShapeDtypeStruct((b * nt, TOP_K, tm), jnp.int32),
                   jax.ShapeDtypeStruct((b * nt, TOP_K, tm), F32),
                   jax.ShapeDtypeStruct((b * nt, TOP_K, tm), jnp.int32),
                   jax.ShapeDtypeStruct((N_EXPERTS, 1), jnp.int32)],
        scratch_shapes=[pltpu.VMEM((N_EXPERTS, 1), F32)],
        compiler_params=_params(("arbitrary", "arbitrary")),
        name="out_projection_router",
    )(o_mla, o_diff, x, mod, w_out_b, g2, wr, br)


def _expert_kernel(be_ref, nr_ref, xs_ref, wg_ref, wu_ref, bg_ref, bu_ref, wd_ref, bd_ref, y_ref):
    i = pl.program_id(0)

    @pl.when(i < nr_ref[0])
    def _():
        xb = xs_ref[...]
        g = jnp.dot(xb, wg_ref[0], preferred_element_type=F32) + bg_ref[0]
        u = jnp.dot(xb, wu_ref[0], preferred_element_type=F32) + bu_ref[0]
        g = jnp.minimum(g, SWIGLU_LIMIT)
        u = jnp.clip(u, -SWIGLU_LIMIT, SWIGLU_LIMIT)
        act = g / (1.0 + jnp.exp(-SWIGLU_ALPHA * g)) * (u + 1.0)
        y_ref[...] = jnp.dot(act.astype(BF16), wd_ref[0], preferred_element_type=F32) + bd_ref[0]

    @pl.when(i >= nr_ref[0])
    def _():
        y_ref[...] = jnp.zeros(y_ref.shape, F32)


def _expert_ffn(blk_expert, n_real, xs, wg, wu, bg, bu, wd, bd):
    p, d = xs.shape
    dff = wg.shape[-1]
    nblk = p // MOE_ROWS
    wmap = lambda i, be, nr: (be[i], 0, 0)
    grid_spec = pltpu.PrefetchScalarGridSpec(
        num_scalar_prefetch=2,
        grid=(nblk,),
        in_specs=[pl.BlockSpec((MOE_ROWS, d), lambda i, be, nr: (i, 0)),
                  pl.BlockSpec((1, d, dff), wmap), pl.BlockSpec((1, d, dff), wmap),
                  pl.BlockSpec((1, 1, dff), wmap), pl.BlockSpec((1, 1, dff), wmap),
                  pl.BlockSpec((1, dff, d), wmap), pl.BlockSpec((1, 1, d), wmap)],
        out_specs=pl.BlockSpec((MOE_ROWS, d), lambda i, be, nr: (i, 0)),
    )
    return pl.pallas_call(
        _expert_kernel,
        grid_spec=grid_spec,
        out_shape=jax.ShapeDtypeStruct((p, d), F32),
        compiler_params=_params(("arbitrary",)),
        name="expert_ffn",
    )(blk_expert, n_real, xs, wg, wu, bg, bu, wd, bd)


def _combine_kernel(yk_ref, tw_ref, x1_ref, mod_ref, gf_ref, o_ref):
    tw = tw_ref[0]
    y = yk_ref[0, 0] * tw[:, 0:1]
    for k in range(1, TOP_K):
        y = y + yk_ref[k, 0] * tw[:, k:k + 1]
    x2 = x1_ref[0] + mod_ref[0][5:6] * y
    o_ref[0] = _rms(x2, gf_ref[...], 1e-6)


def _combine(yk, tw, x1, mod, g_final):
    b, s, d = x1.shape
    tm = min(TOKEN_TILE, s)
    gf = g_final.reshape(1, d)
    return pl.pallas_call(
        _combine_kernel,
        grid=(b, s // tm),
        in_specs=[pl.BlockSpec((TOP_K, 1, tm, d), lambda i, j: (0, i, j, 0)),
                  pl.BlockSpec((1, tm, TOP_K), lambda i, j: (i, j, 0)),
                  pl.BlockSpec((1, tm, d), lambda i, j: (i, j, 0)),
                  pl.BlockSpec((1, N_MOD, d), lambda i, j: (i, 0, 0)),
                  pl.BlockSpec((1, d), lambda i, j: (0, 0))],
        out_specs=pl.BlockSpec((1, tm, d), lambda i, j: (i, j, 0)),
        out_shape=jax.ShapeDtypeStruct((b, s, d), F32),
        compiler_params=_params(("parallel", "parallel")),
        name="combine_final_norm",
    )(yk, tw, x1, mod, gf)


def kernel(x, c, positions, w_ada, b_ada, g_norm1, w_in, g_q_norm, w_uq, g_kv_norm, w_ukv,
           lambda_q1, lambda_k1, lambda_q2, lambda_k2, g_subln, w_out, g_norm2,
           w_router, b_router, w_gate_up, b_gate_up, w_down, b_down, g_final):
    b, s, d = x.shape
    n = b * s
    l = 0
    mod = _ada_mod(c, w_ada[l], b_ada[l]).reshape(b, N_MOD, d)

    w_in_p, w_uq_p, w_ukv_p = _pad_in_weights(w_in[l], w_uq[l], w_ukv[l])
    qm, km, vm, dq, dk, dv = _in_projection(x, positions, mod, g_norm1[l], w_in_p,
                                            g_q_norm[l], w_uq_p, g_kv_norm[l], w_ukv_p)
    o_mla = _mla_attention(qm, km, vm)
    lam_vecs = jnp.stack([lambda_q1[l], lambda_k1[l], lambda_q2[l], lambda_k2[l]]).astype(F32)
    o_diff = _diff_attention(lam_vecs, g_subln[l], dq, dk, dv)

    x1, h2, idx, tw, rank, cnt = _out_projection(o_mla, o_diff, x, mod, w_out[l], g_norm2[l],
                                                 w_router[l], b_router[l])

    to_kn = lambda a: a.transpose(1, 0, 2).reshape(TOP_K, n)
    idx, tw, rank = to_kn(idx), to_kn(tw), to_kn(rank)
    counts = cnt[:, 0]
    padded = (counts + MOE_ROWS - 1) // MOE_ROWS * MOE_ROWS
    end_padded = jnp.cumsum(padded)
    start_padded = end_padded - padded
    slot = start_padded[idx] + rank
    p = n * TOP_K + N_EXPERTS * MOE_ROWS
    nblk = p // MOE_ROWS
    blk_expert = jnp.minimum(
        jnp.searchsorted(end_padded, jnp.arange(nblk, dtype=jnp.int32) * MOE_ROWS, side='right'),
        N_EXPERTS - 1).astype(jnp.int32)
    n_real = (end_padded[-1:] // MOE_ROWS).astype(jnp.int32)
    tok = jnp.broadcast_to(jnp.arange(n, dtype=jnp.int32)[None, :], (TOP_K, n))
    slot_tok = jnp.zeros((p,), jnp.int32).at[slot.reshape(-1)].set(tok.reshape(-1))

    xs = h2.reshape(n, d)[slot_tok]
    gu = w_gate_up[l].reshape(N_EXPERTS, d, EXPERT_DFF, 2)
    bgu = b_gate_up[l].reshape(N_EXPERTS, 1, EXPERT_DFF, 2)
    y = _expert_ffn(blk_expert, n_real, xs,
                    gu[..., 0].astype(BF16), gu[..., 1].astype(BF16), bgu[..., 0], bgu[..., 1],
                    w_down[l].astype(BF16), b_down[l].reshape(N_EXPERTS, 1, d))

    yk = y[slot].reshape(TOP_K, b, s, d)
    return _combine(yk, tw.T.reshape(b, s, TOP_K), x1, mod, g_final)
```

```python
import functools
import math

import jax
import jax.numpy as jnp
from jax import lax
from jax.experimental import pallas as pl
from jax.experimental.pallas import tpu as pltpu

F32 = jnp.float32
BF16 = jnp.bfloat16
HIGHEST = lax.Precision.HIGHEST

D_MODEL = 1024
CHUNK = 64
ROPE_THETA = 500000.0
MLA_HEADS = 8
MLA_NOPE = 64
MLA_ROPE = 32
MLA_V = 64
MLA_QK = MLA_NOPE + MLA_ROPE
Q_RANK = 384
KV_RANK = 256
DIFF_HEADS = 4
DIFF_DIM = 64
DIFF_ROPE = DIFF_DIM // 4
DIFF_WIDTH = DIFF_HEADS * 2 * DIFF_DIM
N_EXPERTS = 32
TOP_K = 4
EXPERT_DFF = D_MODEL
SWIGLU_LIMIT = 7.0
SWIGLU_ALPHA = 1.702
N_MOD = 6
LAMBDA_INIT = 0.8 - 0.6 * math.exp(-0.3 * 0)
LOG2E = math.log2(math.e)

LANES = 128
MXU_COLS = 256
HEAD_PAD = LANES
TOKEN_TILE = 512
ATTN_TILE = 256
MOE_ROWS = 256
VMEM_LIMIT = 48 * 1024 * 1024
EXPERT_VMEM_LIMIT = 58 * 1024 * 1024
NEG_BIG = -1e30

MLA_X1 = 96
MLA_HALF = MLA_ROPE // 2
DIFF_HALF = DIFF_ROPE // 2

C_Q = 0
C_KV = Q_RANK
C_KPE = Q_RANK + KV_RANK
C_DQ = C_KPE + LANES
C_DK = C_DQ + DIFF_WIDTH
IN_COLS_PAD = C_DK + DIFF_WIDTH


def _rms(x, g, eps):
    return x * lax.rsqrt(jnp.mean(x * x, axis=-1, keepdims=True) + eps) * g


def _params(sem, limit=VMEM_LIMIT):
    return pltpu.CompilerParams(dimension_semantics=sem, vmem_limit_bytes=limit)


def _nt(a, b):
    return lax.dot_general(a, b, (((1,), (1,)), ((), ())), preferred_element_type=F32)


def _ada_kernel(c_ref, w_ref, b_ref, o_ref):
    c = c_ref[...]
    sc = c / (1.0 + jnp.exp(-c))
    o_ref[...] = jnp.dot(sc, w_ref[...], preferred_element_type=F32, precision=HIGHEST) + b_ref[...]


def _ada_mod(c, w_ada, b_ada):
    b, d = c.shape
    n = w_ada.shape[1]
    return pl.pallas_call(
        _ada_kernel,
        grid=(n // d,),
        in_specs=[pl.BlockSpec((b, d), lambda j: (0, 0)),
                  pl.BlockSpec((d, d), lambda j: (0, j)),
                  pl.BlockSpec((1, d), lambda j: (0, j))],
        out_specs=pl.BlockSpec((b, d), lambda j: (0, j)),
        out_shape=jax.ShapeDtypeStruct((b, n), F32),
        compiler_params=_params(("arbitrary",)),
        name="ada_mod",
    )(c, w_ada, b_ada.reshape(1, n))


def _rope(t, cos, sin_a, sin_b, shift):
    return (t * cos + pltpu.roll(t, LANES - shift, 1) * sin_a + pltpu.roll(t, shift, 1) * sin_b)


def _inproj_kernel(x_ref, pos_ref, mod_ref, g1_ref, win_ref, wdvt_ref, gq_ref, wuq_ref, gkv_ref,
                   wuk_ref, wuvt_ref, freq_ref, mask_ref,
                   qm_ref, km_ref, vmt_ref, dq_ref, dk_ref, dvt_ref):
    x = x_ref[0]
    mod = mod_ref[0]
    h = (_rms(x, g1_ref[...], 1e-6) * (1.0 + mod[1:2]) + mod[0:1]).astype(BF16)
    proj = jnp.dot(h, win_ref[...], preferred_element_type=F32)
    dvt_ref[0] = _nt(wdvt_ref[...], h).astype(BF16)

    ang = pos_ref[0].astype(F32) * freq_ref[...]
    cos, sin = jnp.cos(ang), jnp.sin(ang)
    mk = mask_ref[...]
    cos_m = jnp.where(mk[0:1] > 0, cos, 1.0)
    sa_m, sb_m = sin * mk[1:2], sin * mk[2:3]
    cos_d = jnp.where(mk[3:4] > 0, cos, 1.0)
    sa_d, sb_d = sin * mk[4:5], sin * mk[5:6]
    mla_scale = MLA_QK ** -0.5 * LOG2E
    diff_scale = DIFF_DIM ** -0.5 * LOG2E

    cq = _rms(proj[:, C_Q:C_Q + Q_RANK], gq_ref[...], 1e-6)
    q = jnp.dot(cq.astype(BF16), wuq_ref[...], preferred_element_type=F32)
    ckv = _rms(proj[:, C_KV:C_KV + KV_RANK], gkv_ref[...], 1e-6).astype(BF16)
    kn = jnp.dot(ckv, wuk_ref[...], preferred_element_type=F32)
    vmt_ref[0] = _nt(wuvt_ref[...], ckv).astype(BF16)
    kpe = _rope(proj[:, C_KPE:C_KPE + LANES], cos_m, sa_m, sb_m, MLA_HALF)
    cq_m, saq_m, sbq_m = cos_m * mla_scale, sa_m * mla_scale, sb_m * mla_scale
    for hd in range(MLA_HEADS):
        sl = slice(hd * HEAD_PAD, (hd + 1) * HEAD_PAD)
        qm_ref[0, :, sl] = _rope(q[:, sl], cq_m, saq_m, sbq_m, MLA_HALF).astype(BF16)
        km_ref[0, :, sl] = (kn[:, sl] + kpe).astype(BF16)

    cq_d, saq_d, sbq_d = cos_d * diff_scale, sa_d * diff_scale, sb_d * diff_scale
    for hd in range(DIFF_HEADS):
        sl = slice(hd * LANES, (hd + 1) * LANES)
        tq = proj[:, C_DQ + hd * LANES:C_DQ + (hd + 1) * LANES]
        tk = proj[:, C_DK + hd * LANES:C_DK + (hd + 1) * LANES]
        dq_ref[0, :, sl] = _rope(tq, cq_d, saq_d, sbq_d, DIFF_HALF).astype(BF16)
        dk_ref[0, :, sl] = _rope(tk, cos_d, sa_d, sb_d, DIFF_HALF).astype(BF16)


def _rope_tables():
    lane = jnp.arange(LANES)
    f_m = ROPE_THETA ** (-jnp.arange(MLA_HALF, dtype=F32) / MLA_HALF)
    f_d = ROPE_THETA ** (-jnp.arange(DIFF_HALF, dtype=F32) / DIFF_HALF)
    m_x1 = (lane >= MLA_X1) & (lane < MLA_X1 + MLA_HALF)
    m_x2 = (lane >= MLA_X1 + MLA_HALF) & (lane < MLA_X1 + 2 * MLA_HALF)
    d_x1 = (lane % DIFF_DIM) < DIFF_HALF
    d_x2 = ((lane % DIFF_DIM) >= DIFF_HALF) & ((lane % DIFF_DIM) < 2 * DIFF_HALF)
    freq = jnp.where(m_x1 | m_x2, f_m[(lane - MLA_X1) % MLA_HALF], 0.0)
    freq = jnp.where(d_x1 | d_x2, f_d[lane % DIFF_HALF], freq)
    z = jnp.zeros((LANES,), F32)
    masks = jnp.stack([(m_x1 | m_x2).astype(F32), -m_x1.astype(F32), m_x2.astype(F32),
                       (d_x1 | d_x2).astype(F32), -d_x1.astype(F32), d_x2.astype(F32), z, z])
    return freq.reshape(1, LANES).astype(F32), masks


def _pad_in_weights(w_in, w_uq, w_ukv):
    d = w_in.shape[0]
    c_kpe = Q_RANK + KV_RANK
    c_dv = c_kpe + MLA_ROPE + 2 * DIFF_WIDTH
    kpe = jnp.concatenate([jnp.zeros((d, MLA_X1), F32), w_in[:, c_kpe:c_kpe + MLA_ROPE]], axis=1)
    w_in_p = jnp.concatenate([w_in[:, :c_kpe], kpe, w_in[:, c_kpe + MLA_ROPE:c_dv]], axis=1)
    w_dvt = w_in[:, c_dv:].T
    uq = w_uq.reshape(Q_RANK, MLA_HEADS, MLA_QK)
    uq_p = jnp.concatenate([uq[:, :, :MLA_NOPE],
                            jnp.zeros((Q_RANK, MLA_HEADS, MLA_X1 - MLA_NOPE), F32),
                            uq[:, :, MLA_NOPE:]], axis=2).reshape(Q_RANK, MLA_HEADS * HEAD_PAD)
    ukv = w_ukv.reshape(KV_RANK, MLA_HEADS, MLA_NOPE + MLA_V)
    uk_p = jnp.concatenate([ukv[:, :, :MLA_NOPE],
                            jnp.zeros((KV_RANK, MLA_HEADS, HEAD_PAD - MLA_NOPE), F32)],
                           axis=2).reshape(KV_RANK, MLA_HEADS * HEAD_PAD)
    uvt = ukv[:, :, MLA_NOPE:].reshape(KV_RANK, MLA_HEADS * MLA_V).T
    return (w_in_p.astype(BF16), w_dvt.astype(BF16), uq_p.astype(BF16), uk_p.astype(BF16),
            uvt.astype(BF16))


def _in_projection(x, positions, mod, g_norm1, w_in_p, w_dvt, g_q, w_uq_p, g_kv, w_uk_p, w_uvt):
    b, s, d = x.shape
    tm = min(TOKEN_TILE, s)
    freq, masks = _rope_tables()
    full = lambda a: pl.BlockSpec(a.shape, lambda i, j: (0,) * a.ndim)
    tile = lambda w: pl.BlockSpec((1, tm, w), lambda i, j: (i, j, 0))
    tile_t = lambda w: pl.BlockSpec((1, w, tm), lambda i, j: (i, 0, j))
    wide, vm_w = MLA_HEADS * HEAD_PAD, MLA_HEADS * MLA_V
    g1, gq, gkv = g_norm1.reshape(1, d), g_q.reshape(1, Q_RANK), g_kv.reshape(1, KV_RANK)
    tok = lambda w: jax.ShapeDtypeStruct((b, s, w), BF16)
    feat = lambda w: jax.ShapeDtypeStruct((b, w, s), BF16)
    return pl.pallas_call(
        _inproj_kernel,
        grid=(b, s // tm),
        in_specs=[tile(d), tile(1), pl.BlockSpec((1, N_MOD, d), lambda i, j: (i, 0, 0)),
                  full(g1), full(w_in_p), full(w_dvt), full(gq), full(w_uq_p), full(gkv),
                  full(w_uk_p), full(w_uvt), full(freq), full(masks)],
        out_specs=[tile(wide), tile(wide), tile_t(vm_w), tile(DIFF_WIDTH), tile(DIFF_WIDTH),
                   tile_t(DIFF_WIDTH)],
        out_shape=[tok(wide), tok(wide), feat(vm_w), tok(DIFF_WIDTH), tok(DIFF_WIDTH),
                   feat(DIFF_WIDTH)],
        compiler_params=_params(("parallel", "parallel")),
        name="in_projection",
    )(x, positions.reshape(b, s, 1), mod, g1, w_in_p, w_dvt, gq, w_uq_p, gkv, w_uk_p, w_uvt,
      freq, masks)


def _diag_mask(t):
    key = lax.broadcasted_iota(jnp.int32, (t, t), 0) // CHUNK
    qry = lax.broadcasted_iota(jnp.int32, (t, t), 1) // CHUNK
    return key <= qry


def _scores_t(q, k_ref, lanes, q0, t):
    diag = jnp.where(_diag_mask(t), _nt(k_ref[0, q0:q0 + t, lanes], q), NEG_BIG)
    bulk = _nt(k_ref[0, 0:q0, lanes], q) if q0 else None
    return bulk, diag


def _softmax_t(bulk, diag):
    m = jnp.max(diag, axis=0, keepdims=True)
    if bulk is not None:
        m = jnp.maximum(m, jnp.max(bulk, axis=0, keepdims=True))
    p_diag = jnp.exp2(diag - m)
    l = jnp.sum(p_diag, axis=0, keepdims=True)
    p_bulk = None
    if bulk is not None:
        p_bulk = jnp.exp2(bulk - m)
        l = l + jnp.sum(p_bulk, axis=0, keepdims=True)
    return p_bulk, p_diag, l


def _pv_t(vt_ref, rows, p_bulk, p_diag, q0, t):
    o = jnp.dot(vt_ref[0, rows, q0:q0 + t], p_diag.astype(BF16), preferred_element_type=F32)
    if p_bulk is not None:
        o = o + jnp.dot(vt_ref[0, rows, 0:q0], p_bulk.astype(BF16), preferred_element_type=F32)
    return o


def _mla_attn_kernel(q_ref, k_ref, vt_ref, o_ref, *, seq, t):
    for q0 in range(0, seq, t):
        outs = []
        for i in range(2):
            lanes = slice(i * HEAD_PAD, (i + 1) * HEAD_PAD)
            bulk, diag = _scores_t(q_ref[0, q0:q0 + t, lanes], k_ref, lanes, q0, t)
            p_bulk, p_diag, l = _softmax_t(bulk, diag)
            rows = slice(i * MLA_V, (i + 1) * MLA_V)
            outs.append(_pv_t(vt_ref, rows, p_bulk, p_diag, q0, t) / l)
        o_ref[0, q0:q0 + t, :] = jnp.concatenate(outs, axis=0).T.astype(BF16)


def _diff_attn_kernel(lam_ref, gs_ref, q_ref, k_ref, vt_ref, o_ref, *, seq, t):
    lv = lam_ref[...]
    lam = (jnp.exp(jnp.sum(lv[0:1] * lv[1:2], axis=-1, keepdims=True))
           - jnp.exp(jnp.sum(lv[2:3] * lv[3:4], axis=-1, keepdims=True)) + LAMBDA_INIT)
    lane = lax.broadcasted_iota(jnp.int32, (t, 2 * DIFF_DIM), 1)
    every = slice(0, 2 * DIFF_DIM)
    for q0 in range(0, seq, t):
        q = q_ref[0, q0:q0 + t, :]
        zero = jnp.zeros_like(q)
        b1, d1 = _scores_t(jnp.where(lane < DIFF_DIM, q, zero), k_ref, every, q0, t)
        b2, d2 = _scores_t(jnp.where(lane >= DIFF_DIM, q, zero), k_ref, every, q0, t)
        pb1, pd1, l1 = _softmax_t(b1, d1)
        pb2, pd2, l2 = _softmax_t(b2, d2)
        c1, c2 = 1.0 / l1, lam / l2
        w_diag = pd1 * c1 - pd2 * c2
        w_bulk = (pb1 * c1 - pb2 * c2) if q0 else None
        o = _pv_t(vt_ref, every, w_bulk, w_diag, q0, t)
        o = (o * lax.rsqrt(jnp.mean(o * o, axis=0, keepdims=True) + 1e-5) * gs_ref[...]
             * (1.0 - LAMBDA_INIT))
        o_ref[0, q0:q0 + t, :] = o.T.astype(BF16)


def _mla_attention(qm, km, vmt):
    b, s, _ = qm.shape
    t = min(ATTN_TILE, s)
    pair = 2 * HEAD_PAD
    return pl.pallas_call(
        functools.partial(_mla_attn_kernel, seq=s, t=t),
        grid=(b, MLA_HEADS // 2),
        in_specs=[pl.BlockSpec((1, s, pair), lambda i, j: (i, 0, j)),
                  pl.BlockSpec((1, s, pair), lambda i, j: (i, 0, j)),
                  pl.BlockSpec((1, LANES, s), lambda i, j: (i, j, 0))],
        out_specs=pl.BlockSpec((1, s, LANES), lambda i, j: (i, 0, j)),
        out_shape=jax.ShapeDtypeStruct((b, s, MLA_HEADS * MLA_V), BF16),
        compiler_params=_params(("parallel", "parallel")),
        name="mla_attention",
    )(qm, km, vmt)


def _diff_attention(lam_vecs, g_subln, dq, dk, dvt):
    b, s, _ = dq.shape
    t = min(ATTN_TILE, s)
    blk = pl.BlockSpec((1, s, LANES), lambda i, j: (i, 0, j))
    gs = g_subln.reshape(2 * DIFF_DIM, 1)
    return pl.pallas_call(
        functools.partial(_diff_attn_kernel, seq=s, t=t),
        grid=(b, DIFF_HEADS),
        in_specs=[pl.BlockSpec(lam_vecs.shape, lambda i, j: (0, 0)),
                  pl.BlockSpec(gs.shape, lambda i, j: (0, 0)), blk, blk,
                  pl.BlockSpec((1, LANES, s), lambda i, j: (i, j, 0))],
        out_specs=blk,
        out_shape=jax.ShapeDtypeStruct((b, s, DIFF_WIDTH), BF16),
        compiler_params=_params(("parallel", "parallel")),
        name="diff_attention",
    )(lam_vecs, gs, dq, dk, dvt)


def _outproj_kernel(om_ref, od_ref, x_ref, mod_ref, wout_ref, g2_ref, wr_ref, br_ref,
                    x1_ref, h2_ref, idx_ref, tw_ref, rank_ref, cnt_ref, carry_ref):
    first = (pl.program_id(0) == 0) & (pl.program_id(1) == 0)

    @pl.when(first)
    def _():
        carry_ref[...] = jnp.zeros(carry_ref.shape, F32)

    half = om_ref.shape[-1]
    mix = (jnp.dot(om_ref[0], wout_ref[:half, :], preferred_element_type=F32)
           + jnp.dot(od_ref[0], wout_ref[half:, :], preferred_element_type=F32))
    mod = mod_ref[0]
    x1 = x_ref[0] + mod[2:3] * mix
    h2 = _rms(x1, g2_ref[...], 1e-6) * (1.0 + mod[4:5]) + mod[3:4]
    x1_ref[0] = x1
    h2_ref[0] = h2.astype(BF16)

    logits = lax.dot_general(wr_ref[...], h2, (((1,), (1,)), ((), ())),
                             preferred_element_type=F32, precision=HIGHEST) + br_ref[...]
    n_e, t = logits.shape
    e_iota = lax.broadcasted_iota(jnp.int32, (n_e, t), 0)
    vals, idxs = [], []
    rest = logits
    for _ in range(TOP_K):
        m = jnp.max(rest, axis=0, keepdims=True)
        ik = jnp.min(jnp.where(rest == m, e_iota, n_e), axis=0, keepdims=True)
        vals.append(m)
        idxs.append(ik)
        rest = jnp.where(e_iota == ik, -jnp.inf, rest)
    ex = [jnp.exp(v - vals[0]) for v in vals]
    den = ex[0] + ex[1] + ex[2] + ex[3]
    sel = (e_iota == idxs[0]) | (e_iota == idxs[1]) | (e_iota == idxs[2]) | (e_iota == idxs[3])

    before = (lax.broadcasted_iota(jnp.int32, (t, t), 0)
              < lax.broadcasted_iota(jnp.int32, (t, t), 1)).astype(BF16)
    prefix = jnp.dot(sel.astype(BF16), before, preferred_element_type=F32)
    pos = carry_ref[...] + prefix
    for k in range(TOP_K):
        idx_ref[0, k:k + 1, :] = idxs[k]
        tw_ref[0, k:k + 1, :] = ex[k] / den
        rk = jnp.sum(jnp.where(e_iota == idxs[k], pos, 0.0), axis=0, keepdims=True)
        rank_ref[0, k:k + 1, :] = rk.astype(jnp.int32)
    total = carry_ref[...] + jnp.sum(sel.astype(F32), axis=1, keepdims=True)
    carry_ref[...] = total
    cnt_ref[...] = total.astype(jnp.int32)


def _out_projection(o_mla, o_diff, x, mod, w_out, g_norm2, w_router, b_router):
    b, s, d = x.shape
    tm = min(TOKEN_TILE, s)
    nt = s // tm
    half = o_mla.shape[-1]
    full = lambda a: pl.BlockSpec(a.shape, lambda i, j: (0,) * a.ndim)
    tile = lambda w: pl.BlockSpec((1, tm, w), lambda i, j: (i, j, 0))
    route = pl.BlockSpec((1, TOP_K, tm), lambda i, j: (i * nt + j, 0, 0))
    g2 = g_norm2.reshape(1, d)
    wr = w_router.T
    br = b_router.reshape(N_EXPERTS, 1)
    w_out_b = w_out.astype(BF16)
    return pl.pallas_call(
        _outproj_kernel,
        grid=(b, nt),
        in_specs=[tile(half), tile(half), tile(d), pl.BlockSpec((1, N_MOD, d), lambda i, j: (i, 0, 0)),
                  full(w_out_b), full(g2), full(wr), full(br)],
        out_specs=[tile(d), tile(d), route, route, route,
                   pl.BlockSpec((N_EXPERTS, 1), lambda i, j: (0, 0))],
        out_shape=[jax.ShapeDtypeStruct((b, s, d), F32), jax.ShapeDtypeStruct((b, s, d), BF16),
                   jax.ShapeDtypeStruct((b * nt, TOP_K, tm), jnp.int32),
                   jax.ShapeDtypeStruct((b * nt, TOP_K, tm), F32),
                   jax.ShapeDtypeStruct((b * nt, TOP_K, tm), jnp.int32),
                   jax.ShapeDtypeStruct((N_EXPERTS, 1), jnp.int32)],
        scratch_shapes=[pltpu.VMEM((N_EXPERTS, 1), F32)],
        compiler_params=_params(("arbitrary", "arbitrary")),
        name="out_projection_router",
    )(o_mla, o_diff, x, mod, w_out_b, g2, wr, br)


def _split_gate_up_perm():
    r = lax.broadcasted_iota(jnp.int32, (MXU_COLS, MXU_COLS), 0)
    c = lax.broadcasted_iota(jnp.int32, (MXU_COLS, MXU_COLS), 1)
    src = jnp.where(c < LANES, 2 * c, 2 * (c - LANES) + 1)
    return (r == src).astype(BF16)


def _expert_kernel(be_ref, nr_ref, xs_ref, wgu_ref, bgu_ref, wd_ref, bd_ref, y_ref,
                   wgu_s, wd_s):
    i = pl.program_id(0)
    real = i < nr_ref[0]
    new_expert = (i == 0) | (be_ref[i] != be_ref[jnp.maximum(i - 1, 0)])
    n_groups = wgu_s.shape[1] // MXU_COLS

    @pl.when(real & new_expert)
    def _():
        perm = _split_gate_up_perm()
        for c in range(n_groups):
            cols = slice(c * MXU_COLS, (c + 1) * MXU_COLS)
            w = wgu_ref[0, :, cols].astype(BF16)
            wgu_s[:, cols] = jnp.dot(w, perm, preferred_element_type=F32).astype(BF16)
        wd_s[...] = wd_ref[0].astype(BF16)

    @pl.when(real)
    def _():
        gu = jnp.dot(xs_ref[...], wgu_s[...], preferred_element_type=F32) + bgu_ref[0]
        acts = []
        for c in range(n_groups):
            g = jnp.minimum(gu[:, c * MXU_COLS:c * MXU_COLS + LANES], SWIGLU_LIMIT)
            u = jnp.clip(gu[:, c * MXU_COLS + LANES:(c + 1) * MXU_COLS], -SWIGLU_LIMIT, SWIGLU_LIMIT)
            acts.append((g / (1.0 + jnp.exp(-SWIGLU_ALPHA * g)) * (u + 1.0)).astype(BF16))
        act = jnp.concatenate(acts, axis=1)
        y_ref[...] = jnp.dot(act, wd_s[...], preferred_element_type=F32) + bd_ref[0]

    @pl.when(jnp.logical_not(real))
    def _():
        y_ref[...] = jnp.zeros(y_ref.shape, F32)


def _expert_ffn(blk_expert, n_real, xs, w_gate_up, b_gate_up_grouped, w_down, b_down):
    p, d = xs.shape
    dff2 = w_gate_up.shape[-1]
    dff = dff2 // 2
    nblk = p // MOE_ROWS
    wmap = lambda i, be, nr: (be[i], 0, 0)
    grid_spec = pltpu.PrefetchScalarGridSpec(
        num_scalar_prefetch=2,
        grid=(nblk,),
        in_specs=[pl.BlockSpec((MOE_ROWS, d), lambda i, be, nr: (i, 0)),
                  pl.BlockSpec((1, d, dff2), wmap), pl.BlockSpec((1, 1, dff2), wmap),
                  pl.BlockSpec((1, dff, d), wmap), pl.BlockSpec((1, 1, d), wmap)],
        out_specs=pl.BlockSpec((MOE_ROWS, d), lambda i, be, nr: (i, 0)),
        scratch_shapes=[pltpu.VMEM((d, dff2), BF16), pltpu.VMEM((dff, d), BF16)],
    )
    return pl.pallas_call(
        _expert_kernel,
        grid_spec=grid_spec,
        out_shape=jax.ShapeDtypeStruct((p, d), F32),
        compiler_params=_params(("arbitrary",), EXPERT_VMEM_LIMIT),
        name="expert_ffn",
    )(blk_expert, n_real, xs, w_gate_up, b_gate_up_grouped, w_down, b_down)


def _combine_kernel(yk_ref, tw_ref, x1_ref, mod_ref, gf_ref, o_ref):
    tw = tw_ref[0]
    y = yk_ref[0, 0] * tw[:, 0:1]
    for k in range(1, TOP_K):
        y = y + yk_ref[k, 0] * tw[:, k:k + 1]
    x2 = x1_ref[0] + mod_ref[0][5:6] * y
    o_ref[0] = _rms(x2, gf_ref[...], 1e-6)


def _combine(yk, tw, x1, mod, g_final):
    b, s, d = x1.shape
    tm = min(TOKEN_TILE, s)
    gf = g_final.reshape(1, d)
    return pl.pallas_call(
        _combine_kernel,
        grid=(b, s // tm),
        in_specs=[pl.BlockSpec((TOP_K, 1, tm, d), lambda i, j: (0, i, j, 0)),
                  pl.BlockSpec((1, tm, TOP_K), lambda i, j: (i, j, 0)),
                  pl.BlockSpec((1, tm, d), lambda i, j: (i, j, 0)),
                  pl.BlockSpec((1, N_MOD, d), lambda i, j: (i, 0, 0)),
                  pl.BlockSpec((1, d), lambda i, j: (0, 0))],
        out_specs=pl.BlockSpec((1, tm, d), lambda i, j: (i, j, 0)),
        out_shape=jax.ShapeDtypeStruct((b, s, d), F32),
        compiler_params=_params(("parallel", "parallel")),
        name="combine_final_norm",
    )(yk, tw, x1, mod, gf)


def kernel(x, c, positions, w_ada, b_ada, g_norm1, w_in, g_q_norm, w_uq, g_kv_norm, w_ukv,
           lambda_q1, lambda_k1, lambda_q2, lambda_k2, g_subln, w_out, g_norm2,
           w_router, b_router, w_gate_up, b_gate_up, w_down, b_down, g_final):
    b, s, d = x.shape
    n = b * s
    l = 0
    mod = _ada_mod(c, w_ada[l], b_ada[l]).reshape(b, N_MOD, d)

    w_in_p, w_dvt, w_uq_p, w_uk_p, w_uvt = _pad_in_weights(w_in[l], w_uq[l], w_ukv[l])
    qm, km, vmt, dq, dk, dvt = _in_projection(x, positions, mod, g_norm1[l], w_in_p, w_dvt,
                                              g_q_norm[l], w_uq_p, g_kv_norm[l], w_uk_p, w_uvt)
    o_mla = _mla_attention(qm, km, vmt)
    lam_vecs = jnp.stack([lambda_q1[l], lambda_k1[l], lambda_q2[l], lambda_k2[l]]).astype(F32)
    o_diff = _diff_attention(lam_vecs, g_subln[l], dq, dk, dvt)

    x1, h2, idx, tw, rank, cnt = _out_projection(o_mla, o_diff, x, mod, w_out[l], g_norm2[l],
                                                 w_router[l], b_router[l])

    to_kn = lambda a: a.transpose(1, 0, 2).reshape(TOP_K, n)
    idx, tw, rank = to_kn(idx), to_kn(tw), to_kn(rank)
    counts = cnt[:, 0]
    padded = (counts + MOE_ROWS - 1) // MOE_ROWS * MOE_ROWS
    end_padded = jnp.cumsum(padded)
    start_padded = end_padded - padded
    slot = start_padded[idx] + rank
    p = n * TOP_K + N_EXPERTS * MOE_ROWS
    nblk = p // MOE_ROWS
    blk_start = jnp.arange(nblk, dtype=jnp.int32) * MOE_ROWS
    blk_expert = jnp.minimum(jnp.sum(end_padded[None, :] <= blk_start[:, None], axis=1),
                             N_EXPERTS - 1).astype(jnp.int32)
    n_real = (end_padded[-1:] // MOE_ROWS).astype(jnp.int32)
    tok = jnp.broadcast_to(jnp.arange(n, dtype=jnp.int32)[None, :], (TOP_K, n))
    slot_tok = jnp.zeros((p,), jnp.int32).at[slot.reshape(-1)].set(tok.reshape(-1))

    xs = h2.reshape(n, d)[slot_tok]
    bgu = (b_gate_up[l].reshape(N_EXPERTS, EXPERT_DFF // LANES, LANES, 2)
           .transpose(0, 1, 3, 2).reshape(N_EXPERTS, 1, 2 * EXPERT_DFF))
    y = _expert_ffn(blk_expert, n_real, xs, w_gate_up[l], bgu, w_down[l],
                    b_down[l].reshape(N_EXPERTS, 1, d))

    yk = y[slot].reshape(TOP_K, b, s, d)
    return _combine(yk, tw.T.reshape(b, s, TOP_K), x1, mod, g_final)
```

```python
import functools
import math

import jax
import jax.numpy as jnp
from jax import lax
from jax.experimental import pallas as pl
from jax.experimental.pallas import tpu as pltpu

F32 = jnp.float32
BF16 = jnp.bfloat16
HIGHEST = lax.Precision.HIGHEST

D_MODEL = 1024
CHUNK = 64
ROPE_THETA = 500000.0
MLA_HEADS = 8
MLA_NOPE = 64
MLA_ROPE = 32
MLA_V = 64
MLA_QK = MLA_NOPE + MLA_ROPE
Q_RANK = 384
KV_RANK = 256
DIFF_HEADS = 4
DIFF_DIM = 64
DIFF_ROPE = DIFF_DIM // 4
DIFF_WIDTH = DIFF_HEADS * 2 * DIFF_DIM
N_EXPERTS = 32
TOP_K = 4
EXPERT_DFF = D_MODEL
SWIGLU_LIMIT = 7.0
SWIGLU_ALPHA = 1.702
N_MOD = 6
LAMBDA_INIT = 0.8 - 0.6 * math.exp(-0.3 * 0)
LOG2E = math.log2(math.e)

LANES = 128
MXU_COLS = 256
HEAD_PAD = LANES
TOKEN_TILE = 512
ATTN_TILE = 256
MOE_ROWS = 256
VMEM_LIMIT = 48 * 1024 * 1024
EXPERT_VMEM_LIMIT = 58 * 1024 * 1024
NEG_BIG = -1e30

MLA_X1 = 96
MLA_HALF = MLA_ROPE // 2
DIFF_HALF = DIFF_ROPE // 2

C_Q = 0
C_KV = Q_RANK
C_KPE = Q_RANK + KV_RANK
C_DQ = C_KPE + LANES
C_DK = C_DQ + DIFF_WIDTH
IN_COLS_PAD = C_DK + DIFF_WIDTH


def _rms(x, g, eps):
    return x * lax.rsqrt(jnp.mean(x * x, axis=-1, keepdims=True) + eps) * g


def _params(sem, limit=VMEM_LIMIT):
    return pltpu.CompilerParams(dimension_semantics=sem, vmem_limit_bytes=limit)


def _nt(a, b):
    return lax.dot_general(a, b, (((1,), (1,)), ((), ())), preferred_element_type=F32)


def _ada_kernel(c_ref, w_ref, b_ref, o_ref):
    c = c_ref[...]
    sc = c / (1.0 + jnp.exp(-c))
    o_ref[...] = jnp.dot(sc, w_ref[...], preferred_element_type=F32, precision=HIGHEST) + b_ref[...]


def _ada_mod(c, w_ada, b_ada):
    b, d = c.shape
    n = w_ada.shape[1]
    return pl.pallas_call(
        _ada_kernel,
        grid=(n // d,),
        in_specs=[pl.BlockSpec((b, d), lambda j: (0, 0)),
                  pl.BlockSpec((d, d), lambda j: (0, j)),
                  pl.BlockSpec((1, d), lambda j: (0, j))],
        out_specs=pl.BlockSpec((b, d), lambda j: (0, j)),
        out_shape=jax.ShapeDtypeStruct((b, n), F32),
        compiler_params=_params(("arbitrary",)),
        name="ada_mod",
    )(c, w_ada, b_ada.reshape(1, n))


def _rope(t, cos, sin_a, sin_b, shift):
    return (t * cos + pltpu.roll(t, LANES - shift, 1) * sin_a + pltpu.roll(t, shift, 1) * sin_b)


def _inproj_kernel(x_ref, pos_ref, mod_ref, g1_ref, win_ref, wdvt_ref, gq_ref, wuq_ref, gkv_ref,
                   wuk_ref, wuvt_ref, freq_ref, mask_ref,
                   qm_ref, km_ref, vmt_ref, dq_ref, dk_ref, dvt_ref):
    x = x_ref[0]
    mod = mod_ref[0]
    h = (_rms(x, g1_ref[...], 1e-6) * (1.0 + mod[1:2]) + mod[0:1]).astype(BF16)
    proj = jnp.dot(h, win_ref[...], preferred_element_type=F32)
    dvt_ref[0] = _nt(wdvt_ref[...], h).astype(BF16)

    ang = pos_ref[0].astype(F32) * freq_ref[...]
    cos, sin = jnp.cos(ang), jnp.sin(ang)
    mk = mask_ref[...]
    cos_m = jnp.where(mk[0:1] > 0, cos, 1.0)
    sa_m, sb_m = sin * mk[1:2], sin * mk[2:3]
    cos_d = jnp.where(mk[3:4] > 0, cos, 1.0)
    sa_d, sb_d = sin * mk[4:5], sin * mk[5:6]
    mla_scale = MLA_QK ** -0.5 * LOG2E
    diff_scale = DIFF_DIM ** -0.5 * LOG2E

    cq = _rms(proj[:, C_Q:C_Q + Q_RANK], gq_ref[...], 1e-6)
    q = jnp.dot(cq.astype(BF16), wuq_ref[...], preferred_element_type=F32)
    ckv = _rms(proj[:, C_KV:C_KV + KV_RANK], gkv_ref[...], 1e-6).astype(BF16)
    kn = jnp.dot(ckv, wuk_ref[...], preferred_element_type=F32)
    vmt_ref[0] = _nt(wuvt_ref[...], ckv).astype(BF16)
    kpe = _rope(proj[:, C_KPE:C_KPE + LANES], cos_m, sa_m, sb_m, MLA_HALF)
    cq_m, saq_m, sbq_m = cos_m * mla_scale, sa_m * mla_scale, sb_m * mla_scale
    for hd in range(MLA_HEADS):
        sl = slice(hd * HEAD_PAD, (hd + 1) * HEAD_PAD)
        qm_ref[0, :, sl] = _rope(q[:, sl], cq_m, saq_m, sbq_m, MLA_HALF).astype(BF16)
        km_ref[0, :, sl] = (kn[:, sl] + kpe).astype(BF16)

    cq_d, saq_d, sbq_d = cos_d * diff_scale, sa_d * diff_scale, sb_d * diff_scale
    for hd in range(DIFF_HEADS):
        sl = slice(hd * LANES, (hd + 1) * LANES)
        tq = proj[:, C_DQ + hd * LANES:C_DQ + (hd + 1) * LANES]
        tk = proj[:, C_DK + hd * LANES:C_DK + (hd + 1) * LANES]
        dq_ref[0, :, sl] = _rope(tq, cq_d, saq_d, sbq_d, DIFF_HALF).astype(BF16)
        dk_ref[0, :, sl] = _rope(tk, cos_d, sa_d, sb_d, DIFF_HALF).astype(BF16)


def _rope_tables():
    lane = jnp.arange(LANES)
    f_m = ROPE_THETA ** (-jnp.arange(MLA_HALF, dtype=F32) / MLA_HALF)
    f_d = ROPE_THETA ** (-jnp.arange(DIFF_HALF, dtype=F32) / DIFF_HALF)
    m_x1 = (lane >= MLA_X1) & (lane < MLA_X1 + MLA_HALF)
    m_x2 = (lane >= MLA_X1 + MLA_HALF) & (lane < MLA_X1 + 2 * MLA_HALF)
    d_x1 = (lane % DIFF_DIM) < DIFF_HALF
    d_x2 = ((lane % DIFF_DIM) >= DIFF_HALF) & ((lane % DIFF_DIM) < 2 * DIFF_HALF)
    freq = jnp.where(m_x1 | m_x2, f_m[(lane - MLA_X1) % MLA_HALF], 0.0)
    freq = jnp.where(d_x1 | d_x2, f_d[lane % DIFF_HALF], freq)
    z = jnp.zeros((LANES,), F32)
    masks = jnp.stack([(m_x1 | m_x2).astype(F32), -m_x1.astype(F32), m_x2.astype(F32),
                       (d_x1 | d_x2).astype(F32), -d_x1.astype(F32), d_x2.astype(F32), z, z])
    return freq.reshape(1, LANES).astype(F32), masks


def _pad_in_weights(w_in, w_uq, w_ukv):
    d = w_in.shape[0]
    c_kpe = Q_RANK + KV_RANK
    c_dv = c_kpe + MLA_ROPE + 2 * DIFF_WIDTH
    kpe = jnp.concatenate([jnp.zeros((d, MLA_X1), F32), w_in[:, c_kpe:c_kpe + MLA_ROPE]], axis=1)
    w_in_p = jnp.concatenate([w_in[:, :c_kpe], kpe, w_in[:, c_kpe + MLA_ROPE:c_dv]], axis=1)
    w_dvt = w_in[:, c_dv:].T
    uq = w_uq.reshape(Q_RANK, MLA_HEADS, MLA_QK)
    uq_p = jnp.concatenate([uq[:, :, :MLA_NOPE],
                            jnp.zeros((Q_RANK, MLA_HEADS, MLA_X1 - MLA_NOPE), F32),
                            uq[:, :, MLA_NOPE:]], axis=2).reshape(Q_RANK, MLA_HEADS * HEAD_PAD)
    ukv = w_ukv.reshape(KV_RANK, MLA_HEADS, MLA_NOPE + MLA_V)
    uk_p = jnp.concatenate([ukv[:, :, :MLA_NOPE],
                            jnp.zeros((KV_RANK, MLA_HEADS, HEAD_PAD - MLA_NOPE), F32)],
                           axis=2).reshape(KV_RANK, MLA_HEADS * HEAD_PAD)
    uvt = ukv[:, :, MLA_NOPE:].reshape(KV_RANK, MLA_HEADS * MLA_V).T
    return (w_in_p.astype(BF16), w_dvt.astype(BF16), uq_p.astype(BF16), uk_p.astype(BF16),
            uvt.astype(BF16))


def _in_projection(x, positions, mod, g_norm1, w_in_p, w_dvt, g_q, w_uq_p, g_kv, w_uk_p, w_uvt):
    b, s, d = x.shape
    tm = min(TOKEN_TILE, s)
    freq, masks = _rope_tables()
    full = lambda a: pl.BlockSpec(a.shape, lambda i, j: (0,) * a.ndim)
    tile = lambda w: pl.BlockSpec((1, tm, w), lambda i, j: (i, j, 0))
    tile_t = lambda w: pl.BlockSpec((1, w, tm), lambda i, j: (i, 0, j))
    wide, vm_w = MLA_HEADS * HEAD_PAD, MLA_HEADS * MLA_V
    g1, gq, gkv = g_norm1.reshape(1, d), g_q.reshape(1, Q_RANK), g_kv.reshape(1, KV_RANK)
    tok = lambda w: jax.ShapeDtypeStruct((b, s, w), BF16)
    feat = lambda w: jax.ShapeDtypeStruct((b, w, s), BF16)
    return pl.pallas_call(
        _inproj_kernel,
        grid=(b, s // tm),
        in_specs=[tile(d), tile(1), pl.BlockSpec((1, N_MOD, d), lambda i, j: (i, 0, 0)),
                  full(g1), full(w_in_p), full(w_dvt), full(gq), full(w_uq_p), full(gkv),
                  full(w_uk_p), full(w_uvt), full(freq), full(masks)],
        out_specs=[tile(wide), tile(wide), tile_t(vm_w), tile(DIFF_WIDTH), tile(DIFF_WIDTH),
                   tile_t(DIFF_WIDTH)],
        out_shape=[tok(wide), tok(wide), feat(vm_w), tok(DIFF_WIDTH), tok(DIFF_WIDTH),
                   feat(DIFF_WIDTH)],
        compiler_params=_params(("parallel", "parallel")),
        name="in_projection",
    )(x, positions.reshape(b, s, 1), mod, g1, w_in_p, w_dvt, gq, w_uq_p, gkv, w_uk_p, w_uvt,
      freq, masks)


def _diag_mask(t):
    key = lax.broadcasted_iota(jnp.int32, (t, t), 0) // CHUNK
    qry = lax.broadcasted_iota(jnp.int32, (t, t), 1) // CHUNK
    return key <= qry


def _scores_t(q, k_ref, lanes, q0, t):
    diag = jnp.where(_diag_mask(t), _nt(k_ref[0, q0:q0 + t, lanes], q), NEG_BIG)
    bulk = _nt(k_ref[0, 0:q0, lanes], q) if q0 else None
    return bulk, diag


def _softmax_t(bulk, diag):
    m = jnp.max(diag, axis=0, keepdims=True)
    if bulk is not None:
        m = jnp.maximum(m, jnp.max(bulk, axis=0, keepdims=True))
    p_diag = jnp.exp2(diag - m)
    l = jnp.sum(p_diag, axis=0, keepdims=True)
    p_bulk = None
    if bulk is not None:
        p_bulk = jnp.exp2(bulk - m)
        l = l + jnp.sum(p_bulk, axis=0, keepdims=True)
    return p_bulk, p_diag, l


def _pv_t(vt_ref, rows, p_bulk, p_diag, q0, t):
    o = jnp.dot(vt_ref[0, rows, q0:q0 + t], p_diag.astype(BF16), preferred_element_type=F32)
    if p_bulk is not None:
        o = o + jnp.dot(vt_ref[0, rows, 0:q0], p_bulk.astype(BF16), preferred_element_type=F32)
    return o


def _mla_attn_kernel(q_ref, k_ref, vt_ref, o_ref, *, seq, t):
    for q0 in range(0, seq, t):
        outs = []
        for i in range(2):
            lanes = slice(i * HEAD_PAD, (i + 1) * HEAD_PAD)
            bulk, diag = _scores_t(q_ref[0, q0:q0 + t, lanes], k_ref, lanes, q0, t)
            p_bulk, p_diag, l = _softmax_t(bulk, diag)
            rows = slice(i * MLA_V, (i + 1) * MLA_V)
            outs.append(_pv_t(vt_ref, rows, p_bulk, p_diag, q0, t) / l)
        o_ref[0, q0:q0 + t, :] = jnp.concatenate(outs, axis=0).T.astype(BF16)


def _diff_attn_kernel(lam_ref, gs_ref, q_ref, k_ref, vt_ref, o_ref, *, seq, t):
    lv = lam_ref[...]
    lam = (jnp.exp(jnp.sum(lv[0:1] * lv[1:2], axis=-1, keepdims=True))
           - jnp.exp(jnp.sum(lv[2:3] * lv[3:4], axis=-1, keepdims=True)) + LAMBDA_INIT)
    lane = lax.broadcasted_iota(jnp.int32, (t, 2 * DIFF_DIM), 1)
    every = slice(0, 2 * DIFF_DIM)
    for q0 in range(0, seq, t):
        q = q_ref[0, q0:q0 + t, :]
        zero = jnp.zeros_like(q)
        b1, d1 = _scores_t(jnp.where(lane < DIFF_DIM, q, zero), k_ref, every, q0, t)
        b2, d2 = _scores_t(jnp.where(lane >= DIFF_DIM, q, zero), k_ref, every, q0, t)
        pb1, pd1, l1 = _softmax_t(b1, d1)
        pb2, pd2, l2 = _softmax_t(b2, d2)
        c1, c2 = 1.0 / l1, lam / l2
        w_diag = pd1 * c1 - pd2 * c2
        w_bulk = (pb1 * c1 - pb2 * c2) if q0 else None
        o = _pv_t(vt_ref, every, w_bulk, w_diag, q0, t)
        o = (o * lax.rsqrt(jnp.mean(o * o, axis=0, keepdims=True) + 1e-5) * gs_ref[...]
             * (1.0 - LAMBDA_INIT))
        o_ref[0, q0:q0 + t, :] = o.T.astype(BF16)


def _mla_attention(qm, km, vmt):
    b, s, _ = qm.shape
    t = min(ATTN_TILE, s)
    pair = 2 * HEAD_PAD
    return pl.pallas_call(
        functools.partial(_mla_attn_kernel, seq=s, t=t),
        grid=(b, MLA_HEADS // 2),
        in_specs=[pl.BlockSpec((1, s, pair), lambda i, j: (i, 0, j)),
                  pl.BlockSpec((1, s, pair), lambda i, j: (i, 0, j)),
                  pl.BlockSpec((1, LANES, s), lambda i, j: (i, j, 0))],
        out_specs=pl.BlockSpec((1, s, LANES), lambda i, j: (i, 0, j)),
        out_shape=jax.ShapeDtypeStruct((b, s, MLA_HEADS * MLA_V), BF16),
        compiler_params=_params(("parallel", "parallel")),
        name="mla_attention",
    )(qm, km, vmt)


def _diff_attention(lam_vecs, g_subln, dq, dk, dvt):
    b, s, _ = dq.shape
    t = min(ATTN_TILE, s)
    blk = pl.BlockSpec((1, s, LANES), lambda i, j: (i, 0, j))
    gs = g_subln.reshape(2 * DIFF_DIM, 1)
    return pl.pallas_call(
        functools.partial(_diff_attn_kernel, seq=s, t=t),
        grid=(b, DIFF_HEADS),
        in_specs=[pl.BlockSpec(lam_vecs.shape, lambda i, j: (0, 0)),
                  pl.BlockSpec(gs.shape, lambda i, j: (0, 0)), blk, blk,
                  pl.BlockSpec((1, LANES, s), lambda i, j: (i, j, 0))],
        out_specs=blk,
        out_shape=jax.ShapeDtypeStruct((b, s, DIFF_WIDTH), BF16),
        compiler_params=_params(("parallel", "parallel")),
        name="diff_attention",
    )(lam_vecs, gs, dq, dk, dvt)


def _outproj_kernel(om_ref, od_ref, x_ref, mod_ref, wout_ref, g2_ref, wr_ref, br_ref,
                    x1_ref, h2_ref, idx_ref, tw_ref, rank_ref, cnt_ref, carry_ref):
    first = (pl.program_id(0) == 0) & (pl.program_id(1) == 0)

    @pl.when(first)
    def _():
        carry_ref[...] = jnp.zeros(carry_ref.shape, F32)

    half = om_ref.shape[-1]
    mix = (jnp.dot(om_ref[0], wout_ref[:half, :], preferred_element_type=F32)
           + jnp.dot(od_ref[0], wout_ref[half:, :], preferred_element_type=F32))
    mod = mod_ref[0]
    x1 = x_ref[0] + mod[2:3] * mix
    h2 = _rms(x1, g2_ref[...], 1e-6) * (1.0 + mod[4:5]) + mod[3:4]
    x1_ref[0] = x1
    h2_ref[0] = h2

    logits = lax.dot_general(wr_ref[...], h2, (((1,), (1,)), ((), ())),
                             preferred_element_type=F32, precision=HIGHEST) + br_ref[...]
    n_e, t = logits.shape
    e_iota = lax.broadcasted_iota(jnp.int32, (n_e, t), 0)
    vals, idxs = [], []
    rest = logits
    for _ in range(TOP_K):
        m = jnp.max(rest, axis=0, keepdims=True)
        ik = jnp.min(jnp.where(rest == m, e_iota, n_e), axis=0, keepdims=True)
        vals.append(m)
        idxs.append(ik)
        rest = jnp.where(e_iota == ik, -jnp.inf, rest)
    ex = [jnp.exp(v - vals[0]) for v in vals]
    den = ex[0] + ex[1] + ex[2] + ex[3]
    sel = (e_iota == idxs[0]) | (e_iota == idxs[1]) | (e_iota == idxs[2]) | (e_iota == idxs[3])

    before = (lax.broadcasted_iota(jnp.int32, (t, t), 0)
              < lax.broadcasted_iota(jnp.int32, (t, t), 1)).astype(BF16)
    prefix = jnp.dot(sel.astype(BF16), before, preferred_element_type=F32)
    pos = carry_ref[...] + prefix
    for k in range(TOP_K):
        idx_ref[0, k:k + 1, :] = idxs[k]
        tw_ref[0, k:k + 1, :] = ex[k] / den
        rk = jnp.sum(jnp.where(e_iota == idxs[k], pos, 0.0), axis=0, keepdims=True)
        rank_ref[0, k:k + 1, :] = rk.astype(jnp.int32)
    total = carry_ref[...] + jnp.sum(sel.astype(F32), axis=1, keepdims=True)
    carry_ref[...] = total
    cnt_ref[...] = total.astype(jnp.int32)


def _out_projection(o_mla, o_diff, x, mod, w_out, g_norm2, w_router, b_router):
    b, s, d = x.shape
    tm = min(TOKEN_TILE, s)
    nt = s // tm
    half = o_mla.shape[-1]
    full = lambda a: pl.BlockSpec(a.shape, lambda i, j: (0,) * a.ndim)
    tile = lambda w: pl.BlockSpec((1, tm, w), lambda i, j: (i, j, 0))
    route = pl.BlockSpec((1, TOP_K, tm), lambda i, j: (i * nt + j, 0, 0))
    g2 = g_norm2.reshape(1, d)
    wr = w_router.T
    br = b_router.reshape(N_EXPERTS, 1)
    w_out_b = w_out.astype(BF16)
    return pl.pallas_call(
        _outproj_kernel,
        grid=(b, nt),
        in_specs=[tile(half), tile(half), tile(d), pl.BlockSpec((1, N_MOD, d), lambda i, j: (i, 0, 0)),
                  full(w_out_b), full(g2), full(wr), full(br)],
        out_specs=[tile(d), tile(d), route, route, route,
                   pl.BlockSpec((N_EXPERTS, 1), lambda i, j: (0, 0))],
        out_shape=[jax.ShapeDtypeStruct((b, s, d), F32), jax.ShapeDtypeStruct((b, s, d), F32),
                   jax.ShapeDtypeStruct((b * nt, TOP_K, tm), jnp.int32),
                   jax.ShapeDtypeStruct((b * nt, TOP_K, tm), F32),
                   jax.ShapeDtypeStruct((b * nt, TOP_K, tm), jnp.int32),
                   jax.ShapeDtypeStruct((N_EXPERTS, 1), jnp.int32)],
        scratch_shapes=[pltpu.VMEM((N_EXPERTS, 1), F32)],
        compiler_params=_params(("arbitrary", "arbitrary")),
        name="out_projection_router",
    )(o_mla, o_diff, x, mod, w_out_b, g2, wr, br)


def _split_gate_up_perm():
    r = lax.broadcasted_iota(jnp.int32, (MXU_COLS, MXU_COLS), 0)
    c = lax.broadcasted_iota(jnp.int32, (MXU_COLS, MXU_COLS), 1)
    src = jnp.where(c < LANES, 2 * c, 2 * (c - LANES) + 1)
    return (r == src).astype(BF16)


def _expert_kernel(be_ref, nr_ref, xs_ref, wgu_ref, bgu_ref, wd_ref, bd_ref, y_ref,
                   wgu_s, wd_s):
    i = pl.program_id(0)
    real = i < nr_ref[0]
    new_expert = (i == 0) | (be_ref[i] != be_ref[jnp.maximum(i - 1, 0)])
    n_groups = wgu_s.shape[1] // MXU_COLS

    @pl.when(real & new_expert)
    def _():
        perm = _split_gate_up_perm()
        for c in range(n_groups):
            cols = slice(c * MXU_COLS, (c + 1) * MXU_COLS)
            w = wgu_ref[0, :, cols].astype(BF16)
            wgu_s[:, cols] = jnp.dot(w, perm, preferred_element_type=F32).astype(BF16)
        wd_s[...] = wd_ref[0].astype(BF16)

    @pl.when(real)
    def _():
        gu = jnp.dot(xs_ref[...].astype(BF16), wgu_s[...], preferred_element_type=F32) + bgu_ref[0]
        acts = []
        for c in range(n_groups):
            g = jnp.minimum(gu[:, c * MXU_COLS:c * MXU_COLS + LANES], SWIGLU_LIMIT)
            u = jnp.clip(gu[:, c * MXU_COLS + LANES:(c + 1) * MXU_COLS], -SWIGLU_LIMIT, SWIGLU_LIMIT)
            acts.append((g / (1.0 + jnp.exp(-SWIGLU_ALPHA * g)) * (u + 1.0)).astype(BF16))
        act = jnp.concatenate(acts, axis=1)
        y_ref[...] = jnp.dot(act, wd_s[...], preferred_element_type=F32) + bd_ref[0]

    @pl.when(jnp.logical_not(real))
    def _():
        y_ref[...] = jnp.zeros(y_ref.shape, F32)


def _expert_ffn(blk_expert, n_real, xs, w_gate_up, b_gate_up_grouped, w_down, b_down):
    p, d = xs.shape
    dff2 = w_gate_up.shape[-1]
    dff = dff2 // 2
    nblk = p // MOE_ROWS
    wmap = lambda i, be, nr: (be[i], 0, 0)
    grid_spec = pltpu.PrefetchScalarGridSpec(
        num_scalar_prefetch=2,
        grid=(nblk,),
        in_specs=[pl.BlockSpec((MOE_ROWS, d), lambda i, be, nr: (i, 0)),
                  pl.BlockSpec((1, d, dff2), wmap), pl.BlockSpec((1, 1, dff2), wmap),
                  pl.BlockSpec((1, dff, d), wmap), pl.BlockSpec((1, 1, d), wmap)],
        out_specs=pl.BlockSpec((MOE_ROWS, d), lambda i, be, nr: (i, 0)),
        scratch_shapes=[pltpu.VMEM((d, dff2), BF16), pltpu.VMEM((dff, d), BF16)],
    )
    return pl.pallas_call(
        _expert_kernel,
        grid_spec=grid_spec,
        out_shape=jax.ShapeDtypeStruct((p, d), F32),
        compiler_params=_params(("arbitrary",), EXPERT_VMEM_LIMIT),
        name="expert_ffn",
    )(blk_expert, n_real, xs, w_gate_up, b_gate_up_grouped, w_down, b_down)


def _combine_kernel(yk_ref, tw_ref, x1_ref, mod_ref, gf_ref, o_ref):
    tw = tw_ref[0]
    y = yk_ref[0, 0] * tw[:, 0:1]
    for k in range(1, TOP_K):
        y = y + yk_ref[k, 0] * tw[:, k:k + 1]
    x2 = x1_ref[0] + mod_ref[0][5:6] * y
    o_ref[0] = _rms(x2, gf_ref[...], 1e-6)


def _combine(yk, tw, x1, mod, g_final):
    b, s, d = x1.shape
    tm = min(TOKEN_TILE, s)
    gf = g_final.reshape(1, d)
    return pl.pallas_call(
        _combine_kernel,
        grid=(b, s // tm),
        in_specs=[pl.BlockSpec((TOP_K, 1, tm, d), lambda i, j: (0, i, j, 0)),
                  pl.BlockSpec((1, tm, TOP_K), lambda i, j: (i, j, 0)),
                  pl.BlockSpec((1, tm, d), lambda i, j: (i, j, 0)),
                  pl.BlockSpec((1, N_MOD, d), lambda i, j: (i, 0, 0)),
                  pl.BlockSpec((1, d), lambda i, j: (0, 0))],
        out_specs=pl.BlockSpec((1, tm, d), lambda i, j: (i, j, 0)),
        out_shape=jax.ShapeDtypeStruct((b, s, d), F32),
        compiler_params=_params(("parallel", "parallel")),
        name="combine_final_norm",
    )(yk, tw, x1, mod, gf)


def kernel(x, c, positions, w_ada, b_ada, g_norm1, w_in, g_q_norm, w_uq, g_kv_norm, w_ukv,
           lambda_q1, lambda_k1, lambda_q2, lambda_k2, g_subln, w_out, g_norm2,
           w_router, b_router, w_gate_up, b_gate_up, w_down, b_down, g_final):
    b, s, d = x.shape
    n = b * s
    l = 0
    mod = _ada_mod(c, w_ada[l], b_ada[l]).reshape(b, N_MOD, d)

    w_in_p, w_dvt, w_uq_p, w_uk_p, w_uvt = _pad_in_weights(w_in[l], w_uq[l], w_ukv[l])
    qm, km, vmt, dq, dk, dvt = _in_projection(x, positions, mod, g_norm1[l], w_in_p, w_dvt,
                                              g_q_norm[l], w_uq_p, g_kv_norm[l], w_uk_p, w_uvt)
    o_mla = _mla_attention(qm, km, vmt)
    lam_vecs = jnp.stack([lambda_q1[l], lambda_k1[l], lambda_q2[l], lambda_k2[l]]).astype(F32)
    o_diff = _diff_attention(lam_vecs, g_subln[l], dq, dk, dvt)

    x1, h2, idx, tw, rank, cnt = _out_projection(o_mla, o_diff, x, mod, w_out[l], g_norm2[l],
                                                 w_router[l], b_router[l])

    to_kn = lambda a: a.transpose(1, 0, 2).reshape(TOP_K, n)
    idx, tw, rank = to_kn(idx), to_kn(tw), to_kn(rank)
    counts = cnt[:, 0]
    padded = (counts + MOE_ROWS - 1) // MOE_ROWS * MOE_ROWS
    end_padded = jnp.cumsum(padded)
    start_padded = end_padded - padded
    e_ids = jnp.arange(N_EXPERTS, dtype=jnp.int32)
    slot = rank + jnp.sum(jnp.where(idx[..., None] == e_ids, start_padded, 0), axis=-1)
    p = n * TOP_K + N_EXPERTS * MOE_ROWS
    nblk = p // MOE_ROWS
    blk_start = jnp.arange(nblk, dtype=jnp.int32) * MOE_ROWS
    blk_expert = jnp.minimum(jnp.sum(end_padded[None, :] <= blk_start[:, None], axis=1),
                             N_EXPERTS - 1).astype(jnp.int32)
    n_real = (end_padded[-1:] // MOE_ROWS).astype(jnp.int32)
    tok = jnp.broadcast_to(jnp.arange(n, dtype=jnp.int32)[None, :], (TOP_K, n))
    slot_tok = jnp.zeros((p,), jnp.int32).at[slot.reshape(-1)].set(
        tok.reshape(-1), unique_indices=True, mode="promise_in_bounds")

    xs = h2.reshape(n, d)[slot_tok]
    bgu = (b_gate_up[l].reshape(N_EXPERTS, EXPERT_DFF // LANES, LANES, 2)
           .transpose(0, 1, 3, 2).reshape(N_EXPERTS, 1, 2 * EXPERT_DFF))
    y = _expert_ffn(blk_expert, n_real, xs, w_gate_up[l], bgu, w_down[l],
                    b_down[l].reshape(N_EXPERTS, 1, d))

    yk = y[slot].reshape(TOP_K, b, s, d)
    return _combine(yk, tw.T.reshape(b, s, TOP_K), x1, mod, g_final)
```

```python
import functools
import math

import jax
import jax.numpy as jnp
from jax import lax
from jax.experimental import pallas as pl
from jax.experimental.pallas import tpu as pltpu

F32 = jnp.float32
BF16 = jnp.bfloat16
HIGHEST = lax.Precision.HIGHEST

D_MODEL = 1024
CHUNK = 64
ROPE_THETA = 500000.0
MLA_HEADS = 8
MLA_NOPE = 64
MLA_ROPE = 32
MLA_V = 64
MLA_QK = MLA_NOPE + MLA_ROPE
Q_RANK = 384
KV_RANK = 256
DIFF_HEADS = 4
DIFF_DIM = 64
DIFF_ROPE = DIFF_DIM // 4
DIFF_WIDTH = DIFF_HEADS * 2 * DIFF_DIM
N_EXPERTS = 32
TOP_K = 4
EXPERT_DFF = D_MODEL
SWIGLU_LIMIT = 7.0
SWIGLU_ALPHA = 1.702
N_MOD = 6
LAMBDA_INIT = 0.8 - 0.6 * math.exp(-0.3 * 0)
LOG2E = math.log2(math.e)

LANES = 128
MXU_COLS = 256
HEAD_PAD = LANES
TOKEN_TILE = 512
ATTN_TILE = 256
MOE_ROWS = 256
VMEM_LIMIT = 48 * 1024 * 1024
EXPERT_VMEM_LIMIT = 58 * 1024 * 1024
NEG_BIG = -1e30

MLA_X1 = 96
MLA_HALF = MLA_ROPE // 2
DIFF_HALF = DIFF_ROPE // 2

C_Q = 0
C_KV = Q_RANK
C_KPE = Q_RANK + KV_RANK
C_DQ = C_KPE + LANES
C_DK = C_DQ + DIFF_WIDTH
IN_COLS_PAD = C_DK + DIFF_WIDTH


def _rms(x, g, eps):
    return x * lax.rsqrt(jnp.mean(x * x, axis=-1, keepdims=True) + eps) * g


def _params(sem, limit=VMEM_LIMIT):
    return pltpu.CompilerParams(dimension_semantics=sem, vmem_limit_bytes=limit)


def _nt(a, b):
    return lax.dot_general(a, b, (((1,), (1,)), ((), ())), preferred_element_type=F32)


def _ada_kernel(c_ref, w_ref, b_ref, o_ref):
    c = c_ref[...]
    sc = c / (1.0 + jnp.exp(-c))
    o_ref[...] = jnp.dot(sc, w_ref[...], preferred_element_type=F32, precision=HIGHEST) + b_ref[...]


def _ada_mod(c, w_ada, b_ada):
    b, d = c.shape
    n = w_ada.shape[1]
    return pl.pallas_call(
        _ada_kernel,
        grid=(n // d,),
        in_specs=[pl.BlockSpec((b, d), lambda j: (0, 0)),
                  pl.BlockSpec((d, d), lambda j: (0, j)),
                  pl.BlockSpec((1, d), lambda j: (0, j))],
        out_specs=pl.BlockSpec((b, d), lambda j: (0, j)),
        out_shape=jax.ShapeDtypeStruct((b, n), F32),
        compiler_params=_params(("arbitrary",)),
        name="ada_mod",
    )(c, w_ada, b_ada.reshape(1, n))


def _rope(t, cos, sin_a, sin_b, shift):
    return (t * cos + pltpu.roll(t, LANES - shift, 1) * sin_a + pltpu.roll(t, shift, 1) * sin_b)


def _inproj_kernel(x_ref, pos_ref, mod_ref, g1_ref, win_ref, wdvt_ref, gq_ref, wuq_ref, gkv_ref,
                   wuk_ref, wuvt_ref, freq_ref, mask_ref,
                   qm_ref, km_ref, vmt_ref, dq_ref, dk_ref, dvt_ref):
    x = x_ref[0]
    mod = mod_ref[0]
    h = (_rms(x, g1_ref[...], 1e-6) * (1.0 + mod[1:2]) + mod[0:1]).astype(BF16)
    proj = jnp.dot(h, win_ref[...], preferred_element_type=F32)
    dvt_ref[0] = _nt(wdvt_ref[...], h).astype(BF16)

    ang = pos_ref[0].astype(F32) * freq_ref[...]
    cos, sin = jnp.cos(ang), jnp.sin(ang)
    mk = mask_ref[...]
    cos_m = jnp.where(mk[0:1] > 0, cos, 1.0)
    sa_m, sb_m = sin * mk[1:2], sin * mk[2:3]
    cos_d = jnp.where(mk[3:4] > 0, cos, 1.0)
    sa_d, sb_d = sin * mk[4:5], sin * mk[5:6]
    mla_scale = MLA_QK ** -0.5 * LOG2E
    diff_scale = DIFF_DIM ** -0.5 * LOG2E

    cq = _rms(proj[:, C_Q:C_Q + Q_RANK], gq_ref[...], 1e-6)
    q = jnp.dot(cq.astype(BF16), wuq_ref[...], preferred_element_type=F32)
    ckv = _rms(proj[:, C_KV:C_KV + KV_RANK], gkv_ref[...], 1e-6).astype(BF16)
    kn = jnp.dot(ckv, wuk_ref[...], preferred_element_type=F32)
    vmt_ref[0] = _nt(wuvt_ref[...], ckv).astype(BF16)
    kpe = _rope(proj[:, C_KPE:C_KPE + LANES], cos_m, sa_m, sb_m, MLA_HALF)
    cq_m, saq_m, sbq_m = cos_m * mla_scale, sa_m * mla_scale, sb_m * mla_scale
    for hd in range(MLA_HEADS):
        sl = slice(hd * HEAD_PAD, (hd + 1) * HEAD_PAD)
        qm_ref[0, :, sl] = _rope(q[:, sl], cq_m, saq_m, sbq_m, MLA_HALF).astype(BF16)
        km_ref[0, :, sl] = (kn[:, sl] + kpe).astype(BF16)

    cq_d, saq_d, sbq_d = cos_d * diff_scale, sa_d * diff_scale, sb_d * diff_scale
    for hd in range(DIFF_HEADS):
        sl = slice(hd * LANES, (hd + 1) * LANES)
        tq = proj[:, C_DQ + hd * LANES:C_DQ + (hd + 1) * LANES]
        tk = proj[:, C_DK + hd * LANES:C_DK + (hd + 1) * LANES]
        dq_ref[0, :, sl] = _rope(tq, cq_d, saq_d, sbq_d, DIFF_HALF).astype(BF16)
        dk_ref[0, :, sl] = _rope(tk, cos_d, sa_d, sb_d, DIFF_HALF).astype(BF16)


def _rope_tables():
    lane = jnp.arange(LANES)
    f_m = ROPE_THETA ** (-jnp.arange(MLA_HALF, dtype=F32) / MLA_HALF)
    f_d = ROPE_THETA ** (-jnp.arange(DIFF_HALF, dtype=F32) / DIFF_HALF)
    m_x1 = (lane >= MLA_X1) & (lane < MLA_X1 + MLA_HALF)
    m_x2 = (lane >= MLA_X1 + MLA_HALF) & (lane < MLA_X1 + 2 * MLA_HALF)
    d_x1 = (lane % DIFF_DIM) < DIFF_HALF
    d_x2 = ((lane % DIFF_DIM) >= DIFF_HALF) & ((lane % DIFF_DIM) < 2 * DIFF_HALF)
    freq = jnp.where(m_x1 | m_x2, f_m[(lane - MLA_X1) % MLA_HALF], 0.0)
    freq = jnp.where(d_x1 | d_x2, f_d[lane % DIFF_HALF], freq)
    z = jnp.zeros((LANES,), F32)
    masks = jnp.stack([(m_x1 | m_x2).astype(F32), -m_x1.astype(F32), m_x2.astype(F32),
                       (d_x1 | d_x2).astype(F32), -d_x1.astype(F32), d_x2.astype(F32), z, z])
    return freq.reshape(1, LANES).astype(F32), masks


def _pad_in_weights(w_in, w_uq, w_ukv):
    d = w_in.shape[0]
    c_kpe = Q_RANK + KV_RANK
    c_dv = c_kpe + MLA_ROPE + 2 * DIFF_WIDTH
    kpe = jnp.concatenate([jnp.zeros((d, MLA_X1), F32), w_in[:, c_kpe:c_kpe + MLA_ROPE]], axis=1)
    w_in_p = jnp.concatenate([w_in[:, :c_kpe], kpe, w_in[:, c_kpe + MLA_ROPE:c_dv]], axis=1)
    w_dvt = w_in[:, c_dv:].T
    uq = w_uq.reshape(Q_RANK, MLA_HEADS, MLA_QK)
    uq_p = jnp.concatenate([uq[:, :, :MLA_NOPE],
                            jnp.zeros((Q_RANK, MLA_HEADS, MLA_X1 - MLA_NOPE), F32),
                            uq[:, :, MLA_NOPE:]], axis=2).reshape(Q_RANK, MLA_HEADS * HEAD_PAD)
    ukv = w_ukv.reshape(KV_RANK, MLA_HEADS, MLA_NOPE + MLA_V)
    uk_p = jnp.concatenate([ukv[:, :, :MLA_NOPE],
                            jnp.zeros((KV_RANK, MLA_HEADS, HEAD_PAD - MLA_NOPE), F32)],
                           axis=2).reshape(KV_RANK, MLA_HEADS * HEAD_PAD)
    uvt = ukv[:, :, MLA_NOPE:].reshape(KV_RANK, MLA_HEADS * MLA_V).T
    return (w_in_p.astype(BF16), w_dvt.astype(BF16), uq_p.astype(BF16), uk_p.astype(BF16),
            uvt.astype(BF16))


def _in_projection(x, positions, mod, g_norm1, w_in_p, w_dvt, g_q, w_uq_p, g_kv, w_uk_p, w_uvt):
    b, s, d = x.shape
    tm = min(TOKEN_TILE, s)
    freq, masks = _rope_tables()
    full = lambda a: pl.BlockSpec(a.shape, lambda i, j: (0,) * a.ndim)
    tile = lambda w: pl.BlockSpec((1, tm, w), lambda i, j: (i, j, 0))
    tile_t = lambda w: pl.BlockSpec((1, w, tm), lambda i, j: (i, 0, j))
    wide, vm_w = MLA_HEADS * HEAD_PAD, MLA_HEADS * MLA_V
    g1, gq, gkv = g_norm1.reshape(1, d), g_q.reshape(1, Q_RANK), g_kv.reshape(1, KV_RANK)
    tok = lambda w: jax.ShapeDtypeStruct((b, s, w), BF16)
    feat = lambda w: jax.ShapeDtypeStruct((b, w, s), BF16)
    return pl.pallas_call(
        _inproj_kernel,
        grid=(b, s // tm),
        in_specs=[tile(d), tile(1), pl.BlockSpec((1, N_MOD, d), lambda i, j: (i, 0, 0)),
                  full(g1), full(w_in_p), full(w_dvt), full(gq), full(w_uq_p), full(gkv),
                  full(w_uk_p), full(w_uvt), full(freq), full(masks)],
        out_specs=[tile(wide), tile(wide), tile_t(vm_w), tile(DIFF_WIDTH), tile(DIFF_WIDTH),
                   tile_t(DIFF_WIDTH)],
        out_shape=[tok(wide), tok(wide), feat(vm_w), tok(DIFF_WIDTH), tok(DIFF_WIDTH),
                   feat(DIFF_WIDTH)],
        compiler_params=_params(("parallel", "parallel")),
        name="in_projection",
    )(x, positions.reshape(b, s, 1), mod, g1, w_in_p, w_dvt, gq, w_uq_p, gkv, w_uk_p, w_uvt,
      freq, masks)


def _diag_mask(t):
    key = lax.broadcasted_iota(jnp.int32, (t, t), 0) // CHUNK
    qry = lax.broadcasted_iota(jnp.int32, (t, t), 1) // CHUNK
    return key <= qry


def _scores_t(q, k_ref, lanes, q0, t):
    diag = jnp.where(_diag_mask(t), _nt(k_ref[0, q0:q0 + t, lanes], q), NEG_BIG)
    bulk = _nt(k_ref[0, 0:q0, lanes], q) if q0 else None
    return bulk, diag


def _softmax_t(bulk, diag):
    m = jnp.max(diag, axis=0, keepdims=True)
    if bulk is not None:
        m = jnp.maximum(m, jnp.max(bulk, axis=0, keepdims=True))
    p_diag = jnp.exp2(diag - m)
    l = jnp.sum(p_diag, axis=0, keepdims=True)
    p_bulk = None
    if bulk is not None:
        p_bulk = jnp.exp2(bulk - m)
        l = l + jnp.sum(p_bulk, axis=0, keepdims=True)
    return p_bulk, p_diag, l


def _pv_t(vt_ref, rows, p_bulk, p_diag, q0, t):
    o = jnp.dot(vt_ref[0, rows, q0:q0 + t], p_diag.astype(BF16), preferred_element_type=F32)
    if p_bulk is not None:
        o = o + jnp.dot(vt_ref[0, rows, 0:q0], p_bulk.astype(BF16), preferred_element_type=F32)
    return o


def _mla_attn_kernel(q_ref, k_ref, vt_ref, o_ref, *, seq, t):
    for q0 in range(0, seq, t):
        outs = []
        for i in range(2):
            lanes = slice(i * HEAD_PAD, (i + 1) * HEAD_PAD)
            bulk, diag = _scores_t(q_ref[0, q0:q0 + t, lanes], k_ref, lanes, q0, t)
            p_bulk, p_diag, l = _softmax_t(bulk, diag)
            rows = slice(i * MLA_V, (i + 1) * MLA_V)
            outs.append(_pv_t(vt_ref, rows, p_bulk, p_diag, q0, t) / l)
        o_ref[0, q0:q0 + t, :] = jnp.concatenate(outs, axis=0).T.astype(BF16)


def _diff_attn_kernel(lam_ref, gs_ref, q_ref, k_ref, vt_ref, o_ref, *, seq, t):
    lv = lam_ref[...]
    lam = (jnp.exp(jnp.sum(lv[0:1] * lv[1:2], axis=-1, keepdims=True))
           - jnp.exp(jnp.sum(lv[2:3] * lv[3:4], axis=-1, keepdims=True)) + LAMBDA_INIT)
    lane = lax.broadcasted_iota(jnp.int32, (t, 2 * DIFF_DIM), 1)
    every = slice(0, 2 * DIFF_DIM)
    for q0 in range(0, seq, t):
        q = q_ref[0, q0:q0 + t, :]
        zero = jnp.zeros_like(q)
        b1, d1 = _scores_t(jnp.where(lane < DIFF_DIM, q, zero), k_ref, every, q0, t)
        b2, d2 = _scores_t(jnp.where(lane >= DIFF_DIM, q, zero), k_ref, every, q0, t)
        pb1, pd1, l1 = _softmax_t(b1, d1)
        pb2, pd2, l2 = _softmax_t(b2, d2)
        c1, c2 = 1.0 / l1, lam / l2
        w_diag = pd1 * c1 - pd2 * c2
        w_bulk = (pb1 * c1 - pb2 * c2) if q0 else None
        o = _pv_t(vt_ref, every, w_bulk, w_diag, q0, t)
        o = (o * lax.rsqrt(jnp.mean(o * o, axis=0, keepdims=True) + 1e-5) * gs_ref[...]
             * (1.0 - LAMBDA_INIT))
        o_ref[0, q0:q0 + t, :] = o.T.astype(BF16)


def _mla_attention(qm, km, vmt):
    b, s, _ = qm.shape
    t = min(ATTN_TILE, s)
    pair = 2 * HEAD_PAD
    return pl.pallas_call(
        functools.partial(_mla_attn_kernel, seq=s, t=t),
        grid=(b, MLA_HEADS // 2),
        in_specs=[pl.BlockSpec((1, s, pair), lambda i, j: (i, 0, j)),
                  pl.BlockSpec((1, s, pair), lambda i, j: (i, 0, j)),
                  pl.BlockSpec((1, LANES, s), lambda i, j: (i, j, 0))],
        out_specs=pl.BlockSpec((1, s, LANES), lambda i, j: (i, 0, j)),
        out_shape=jax.ShapeDtypeStruct((b, s, MLA_HEADS * MLA_V), BF16),
        compiler_params=_params(("parallel", "parallel")),
        name="mla_attention",
    )(qm, km, vmt)


def _diff_attention(lam_vecs, g_subln, dq, dk, dvt):
    b, s, _ = dq.shape
    t = min(ATTN_TILE, s)
    blk = pl.BlockSpec((1, s, LANES), lambda i, j: (i, 0, j))
    gs = g_subln.reshape(2 * DIFF_DIM, 1)
    return pl.pallas_call(
        functools.partial(_diff_attn_kernel, seq=s, t=t),
        grid=(b, DIFF_HEADS),
        in_specs=[pl.BlockSpec(lam_vecs.shape, lambda i, j: (0, 0)),
                  pl.BlockSpec(gs.shape, lambda i, j: (0, 0)), blk, blk,
                  pl.BlockSpec((1, LANES, s), lambda i, j: (i, j, 0))],
        out_specs=blk,
        out_shape=jax.ShapeDtypeStruct((b, s, DIFF_WIDTH), BF16),
        compiler_params=_params(("parallel", "parallel")),
        name="diff_attention",
    )(lam_vecs, gs, dq, dk, dvt)


def _outproj_kernel(om_ref, od_ref, x_ref, mod_ref, wout_ref, g2_ref, wr_ref, br_ref,
                    x1_ref, h2_ref, idx_ref, tw_ref, rank_ref, cnt_ref, carry_ref):
    first = (pl.program_id(0) == 0) & (pl.program_id(1) == 0)

    @pl.when(first)
    def _():
        carry_ref[...] = jnp.zeros(carry_ref.shape, F32)

    half = om_ref.shape[-1]
    mix = (jnp.dot(om_ref[0], wout_ref[:half, :], preferred_element_type=F32)
           + jnp.dot(od_ref[0], wout_ref[half:, :], preferred_element_type=F32))
    mod = mod_ref[0]
    x1 = x_ref[0] + mod[2:3] * mix
    h2 = _rms(x1, g2_ref[...], 1e-6) * (1.0 + mod[4:5]) + mod[3:4]
    x1_ref[0] = x1
    h2_ref[0] = h2

    logits = lax.dot_general(wr_ref[...], h2, (((1,), (1,)), ((), ())),
                             preferred_element_type=F32, precision=HIGHEST) + br_ref[...]
    n_e, t = logits.shape
    e_iota = lax.broadcasted_iota(jnp.int32, (n_e, t), 0)
    vals, idxs = [], []
    rest = logits
    for _ in range(TOP_K):
        m = jnp.max(rest, axis=0, keepdims=True)
        ik = jnp.min(jnp.where(rest == m, e_iota, n_e), axis=0, keepdims=True)
        vals.append(m)
        idxs.append(ik)
        rest = jnp.where(e_iota == ik, -jnp.inf, rest)
    ex = [jnp.exp(v - vals[0]) for v in vals]
    den = ex[0] + ex[1] + ex[2] + ex[3]
    sel = (e_iota == idxs[0]) | (e_iota == idxs[1]) | (e_iota == idxs[2]) | (e_iota == idxs[3])

    before = (lax.broadcasted_iota(jnp.int32, (t, t), 0)
              < lax.broadcasted_iota(jnp.int32, (t, t), 1)).astype(BF16)
    prefix = jnp.dot(sel.astype(BF16), before, preferred_element_type=F32)
    pos = carry_ref[...] + prefix
    for k in range(TOP_K):
        idx_ref[0, k:k + 1, :] = idxs[k]
        tw_ref[0, k:k + 1, :] = ex[k] / den
        rk = jnp.sum(jnp.where(e_iota == idxs[k], pos, 0.0), axis=0, keepdims=True)
        rank_ref[0, k:k + 1, :] = rk.astype(jnp.int32)
    total = carry_ref[...] + jnp.sum(sel.astype(F32), axis=1, keepdims=True)
    carry_ref[...] = total
    cnt_ref[...] = total.astype(jnp.int32)


def _out_projection(o_mla, o_diff, x, mod, w_out, g_norm2, w_router, b_router):
    b, s, d = x.shape
    tm = min(TOKEN_TILE, s)
    nt = s // tm
    half = o_mla.shape[-1]
    full = lambda a: pl.BlockSpec(a.shape, lambda i, j: (0,) * a.ndim)
    tile = lambda w: pl.BlockSpec((1, tm, w), lambda i, j: (i, j, 0))
    route = pl.BlockSpec((1, TOP_K, tm), lambda i, j: (i * nt + j, 0, 0))
    g2 = g_norm2.reshape(1, d)
    wr = w_router.T
    br = b_router.reshape(N_EXPERTS, 1)
    w_out_b = w_out.astype(BF16)
    return pl.pallas_call(
        _outproj_kernel,
        grid=(b, nt),
        in_specs=[tile(half), tile(half), tile(d), pl.BlockSpec((1, N_MOD, d), lambda i, j: (i, 0, 0)),
                  full(w_out_b), full(g2), full(wr), full(br)],
        out_specs=[tile(d), tile(d), route, route, route,
                   pl.BlockSpec((N_EXPERTS, 1), lambda i, j: (0, 0))],
        out_shape=[jax.ShapeDtypeStruct((b, s, d), F32), jax.ShapeDtypeStruct((b, s, d), F32),
                   jax.ShapeDtypeStruct((b * nt, TOP_K, tm), jnp.int32),
                   jax.ShapeDtypeStruct((b * nt, TOP_K, tm), F32),
                   jax.ShapeDtypeStruct((b * nt, TOP_K, tm), jnp.int32),
                   jax.ShapeDtypeStruct((N_EXPERTS, 1), jnp.int32)],
        scratch_shapes=[pltpu.VMEM((N_EXPERTS, 1), F32)],
        compiler_params=_params(("arbitrary", "arbitrary")),
        name="out_projection_router",
    )(o_mla, o_diff, x, mod, w_out_b, g2, wr, br)


def _split_gate_up_perm():
    r = lax.broadcasted_iota(jnp.int32, (MXU_COLS, MXU_COLS), 0)
    c = lax.broadcasted_iota(jnp.int32, (MXU_COLS, MXU_COLS), 1)
    src = jnp.where(c < LANES, 2 * c, 2 * (c - LANES) + 1)
    return (r == src).astype(BF16)


def _expert_kernel(ie_ref, ib_ref, lo_ref, hi_ref, nv_ref, xs_ref, wgu_ref, bgu_ref, wd_ref, bd_ref,
                   y_ref, wgu_s, wd_s, y_s):
    i = pl.program_id(0)
    prev = jnp.maximum(i - 1, 0)
    valid = i < nv_ref[0]
    new_expert = (i == 0) | (ie_ref[i] != ie_ref[prev])
    new_block = (i == 0) | (ib_ref[i] != ib_ref[prev])
    n_groups = wgu_s.shape[1] // MXU_COLS
    row = lax.broadcasted_iota(jnp.int32, (y_ref.shape[0], 1), 0)
    mine = (row >= lo_ref[i]) & (row < hi_ref[i])

    @pl.when(valid & new_expert)
    def _():
        perm = _split_gate_up_perm()
        for c in range(n_groups):
            cols = slice(c * MXU_COLS, (c + 1) * MXU_COLS)
            w = wgu_ref[0, :, cols].astype(BF16)
            wgu_s[:, cols] = jnp.dot(w, perm, preferred_element_type=F32).astype(BF16)
        wd_s[...] = wd_ref[0].astype(BF16)

    @pl.when(valid)
    def _():
        gu = jnp.dot(xs_ref[...].astype(BF16), wgu_s[...], preferred_element_type=F32) + bgu_ref[0]
        acts = []
        for c in range(n_groups):
            g = jnp.minimum(gu[:, c * MXU_COLS:c * MXU_COLS + LANES], SWIGLU_LIMIT)
            u = jnp.clip(gu[:, c * MXU_COLS + LANES:(c + 1) * MXU_COLS], -SWIGLU_LIMIT, SWIGLU_LIMIT)
            acts.append((g / (1.0 + jnp.exp(-SWIGLU_ALPHA * g)) * (u + 1.0)).astype(BF16))
        act = jnp.concatenate(acts, axis=1)
        y_s[...] = jnp.dot(act, wd_s[...], preferred_element_type=F32) + bd_ref[0]

    @pl.when(valid & new_block)
    def _():
        y_ref[...] = jnp.where(mine, y_s[...], 0.0)

    @pl.when(valid & jnp.logical_not(new_block))
    def _():
        y_ref[...] = jnp.where(mine, y_s[...], y_ref[...])


def _expert_ffn(items, xs, w_gate_up, b_gate_up_grouped, w_down, b_down):
    a, d = xs.shape
    dff2 = w_gate_up.shape[-1]
    dff = dff2 // 2
    n_items = items[0].shape[0]
    wmap = lambda i, ie, ib, lo, hi, nv: (ie[i], 0, 0)
    rmap = lambda i, ie, ib, lo, hi, nv: (ib[i], 0)
    grid_spec = pltpu.PrefetchScalarGridSpec(
        num_scalar_prefetch=5,
        grid=(n_items,),
        in_specs=[pl.BlockSpec((MOE_ROWS, d), rmap),
                  pl.BlockSpec((1, d, dff2), wmap), pl.BlockSpec((1, 1, dff2), wmap),
                  pl.BlockSpec((1, dff, d), wmap), pl.BlockSpec((1, 1, d), wmap)],
        out_specs=pl.BlockSpec((MOE_ROWS, d), rmap),
        scratch_shapes=[pltpu.VMEM((d, dff2), BF16), pltpu.VMEM((dff, d), BF16),
                        pltpu.VMEM((MOE_ROWS, d), F32)],
    )
    return pl.pallas_call(
        _expert_kernel,
        grid_spec=grid_spec,
        out_shape=jax.ShapeDtypeStruct((a, d), F32),
        compiler_params=_params(("arbitrary",), EXPERT_VMEM_LIMIT),
        name="expert_ffn",
    )(*items, xs, w_gate_up, b_gate_up_grouped, w_down, b_down)


def _expert_work_items(counts, n_rows):
    e_ids = jnp.arange(N_EXPERTS, dtype=jnp.int32)
    ends = jnp.cumsum(counts).astype(jnp.int32)
    starts = ends - counts
    first_blk = starts // MOE_ROWS
    last_blk = (ends - 1) // MOE_ROWS
    per_expert = jnp.where(counts > 0, last_blk - first_blk + 1, 0)
    item_end = jnp.cumsum(per_expert).astype(jnp.int32)
    item_start = item_end - per_expert
    total = item_end[-1:]
    max_items = n_rows // MOE_ROWS + N_EXPERTS - 1
    it = jnp.minimum(jnp.arange(max_items, dtype=jnp.int32), total - 1)
    expert = jnp.sum(item_end[None, :] <= it[:, None], axis=1).astype(jnp.int32)
    pick = lambda v: jnp.sum(jnp.where(expert[:, None] == e_ids, v, 0), axis=1)
    block = pick(first_blk) + it - pick(item_start)
    lo = jnp.maximum(pick(starts) - block * MOE_ROWS, 0)
    hi = jnp.minimum(pick(ends) - block * MOE_ROWS, MOE_ROWS)
    return expert, block, lo, hi, total, starts


def _combine_kernel(yk_ref, tw_ref, x1_ref, mod_ref, gf_ref, o_ref):
    tw = tw_ref[0]
    y = yk_ref[0, 0] * tw[:, 0:1]
    for k in range(1, TOP_K):
        y = y + yk_ref[k, 0] * tw[:, k:k + 1]
    x2 = x1_ref[0] + mod_ref[0][5:6] * y
    o_ref[0] = _rms(x2, gf_ref[...], 1e-6)


def _combine(yk, tw, x1, mod, g_final):
    b, s, d = x1.shape
    tm = min(TOKEN_TILE, s)
    gf = g_final.reshape(1, d)
    return pl.pallas_call(
        _combine_kernel,
        grid=(b, s // tm),
        in_specs=[pl.BlockSpec((TOP_K, 1, tm, d), lambda i, j: (0, i, j, 0)),
                  pl.BlockSpec((1, tm, TOP_K), lambda i, j: (i, j, 0)),
                  pl.BlockSpec((1, tm, d), lambda i, j: (i, j, 0)),
                  pl.BlockSpec((1, N_MOD, d), lambda i, j: (i, 0, 0)),
                  pl.BlockSpec((1, d), lambda i, j: (0, 0))],
        out_specs=pl.BlockSpec((1, tm, d), lambda i, j: (i, j, 0)),
        out_shape=jax.ShapeDtypeStruct((b, s, d), F32),
        compiler_params=_params(("parallel", "parallel")),
        name="combine_final_norm",
    )(yk, tw, x1, mod, gf)


def kernel(x, c, positions, w_ada, b_ada, g_norm1, w_in, g_q_norm, w_uq, g_kv_norm, w_ukv,
           lambda_q1, lambda_k1, lambda_q2, lambda_k2, g_subln, w_out, g_norm2,
           w_router, b_router, w_gate_up, b_gate_up, w_down, b_down, g_final):
    b, s, d = x.shape
    n = b * s
    l = 0
    mod = _ada_mod(c, w_ada[l], b_ada[l]).reshape(b, N_MOD, d)

    w_in_p, w_dvt, w_uq_p, w_uk_p, w_uvt = _pad_in_weights(w_in[l], w_uq[l], w_ukv[l])
    qm, km, vmt, dq, dk, dvt = _in_projection(x, positions, mod, g_norm1[l], w_in_p, w_dvt,
                                              g_q_norm[l], w_uq_p, g_kv_norm[l], w_uk_p, w_uvt)
    o_mla = _mla_attention(qm, km, vmt)
    lam_vecs = jnp.stack([lambda_q1[l], lambda_k1[l], lambda_q2[l], lambda_k2[l]]).astype(F32)
    o_diff = _diff_attention(lam_vecs, g_subln[l], dq, dk, dvt)

    x1, h2, idx, tw, rank, cnt = _out_projection(o_mla, o_diff, x, mod, w_out[l], g_norm2[l],
                                                 w_router[l], b_router[l])

    to_kn = lambda a: a.transpose(1, 0, 2).reshape(TOP_K, n)
    idx, tw, rank = to_kn(idx), to_kn(tw), to_kn(rank)
    n_rows = n * TOP_K
    *items, starts = _expert_work_items(cnt[:, 0], n_rows)
    e_ids = jnp.arange(N_EXPERTS, dtype=jnp.int32)
    slot = rank + jnp.sum(jnp.where(idx[..., None] == e_ids, starts, 0), axis=-1)
    tok = jnp.arange(n, dtype=jnp.int32)
    row_tok = jnp.sort((idx * n + tok).reshape(-1)) % n

    xs = h2.reshape(n, d)[row_tok]
    bgu = (b_gate_up[l].reshape(N_EXPERTS, EXPERT_DFF // LANES, LANES, 2)
           .transpose(0, 1, 3, 2).reshape(N_EXPERTS, 1, 2 * EXPERT_DFF))
    y = _expert_ffn(items, xs, w_gate_up[l], bgu, w_down[l], b_down[l].reshape(N_EXPERTS, 1, d))

    yk = y[slot].reshape(TOP_K, b, s, d)
    return _combine(yk, tw.T.reshape(b, s, TOP_K), x1, mod, g_final)
```

```python
import functools
import math

import jax
import jax.numpy as jnp
from jax import lax
from jax.experimental import pallas as pl
from jax.experimental.pallas import tpu as pltpu
from jax.experimental.pallas import tpu_sc as plsc

F32 = jnp.float32
BF16 = jnp.bfloat16
HIGHEST = lax.Precision.HIGHEST

D_MODEL = 1024
CHUNK = 64
ROPE_THETA = 500000.0
MLA_HEADS = 8
MLA_NOPE = 64
MLA_ROPE = 32
MLA_V = 64
MLA_QK = MLA_NOPE + MLA_ROPE
Q_RANK = 384
KV_RANK = 256
DIFF_HEADS = 4
DIFF_DIM = 64
DIFF_ROPE = DIFF_DIM // 4
DIFF_WIDTH = DIFF_HEADS * 2 * DIFF_DIM
N_EXPERTS = 32
TOP_K = 4
EXPERT_DFF = D_MODEL
SWIGLU_LIMIT = 7.0
SWIGLU_ALPHA = 1.702
N_MOD = 6
LAMBDA_INIT = 0.8 - 0.6 * math.exp(-0.3 * 0)
LOG2E = math.log2(math.e)

LANES = 128
MXU_COLS = 256
HEAD_PAD = LANES
TOKEN_TILE = 512
ATTN_TILE = 256
MOE_ROWS = 256
GATHER_WINDOW = 128
STRIP = 256
VMEM_LIMIT = 48 * 1024 * 1024
EXPERT_VMEM_LIMIT = 58 * 1024 * 1024
NEG_BIG = -1e30

MLA_X1 = 96
MLA_HALF = MLA_ROPE // 2
DIFF_HALF = DIFF_ROPE // 2

C_Q = 0
C_KV = Q_RANK
C_KPE = Q_RANK + KV_RANK
C_DQ = C_KPE + LANES
C_DK = C_DQ + DIFF_WIDTH
IN_COLS_PAD = C_DK + DIFF_WIDTH


def _rms(x, g, eps):
    return x * lax.rsqrt(jnp.mean(x * x, axis=-1, keepdims=True) + eps) * g


def _params(sem, limit=VMEM_LIMIT):
    return pltpu.CompilerParams(dimension_semantics=sem, vmem_limit_bytes=limit)


def _nt(a, b):
    return lax.dot_general(a, b, (((1,), (1,)), ((), ())), preferred_element_type=F32)


def _ada_kernel(c_ref, w_ref, b_ref, o_ref):
    c = c_ref[...]
    sc = c / (1.0 + jnp.exp(-c))
    o_ref[...] = jnp.dot(sc, w_ref[...], preferred_element_type=F32, precision=HIGHEST) + b_ref[...]


def _ada_mod(c, w_ada, b_ada):
    b, d = c.shape
    n = w_ada.shape[1]
    return pl.pallas_call(
        _ada_kernel,
        grid=(n // d,),
        in_specs=[pl.BlockSpec((b, d), lambda j: (0, 0)),
                  pl.BlockSpec((d, d), lambda j: (0, j)),
                  pl.BlockSpec((1, d), lambda j: (0, j))],
        out_specs=pl.BlockSpec((b, d), lambda j: (0, j)),
        out_shape=jax.ShapeDtypeStruct((b, n), F32),
        compiler_params=_params(("arbitrary",)),
        name="ada_mod",
    )(c, w_ada, b_ada.reshape(1, n))


def _rope(t, cos, sin_a, sin_b, shift):
    return (t * cos + pltpu.roll(t, LANES - shift, 1) * sin_a + pltpu.roll(t, shift, 1) * sin_b)


def _inproj_kernel(x_ref, pos_ref, mod_ref, g1_ref, win_ref, wdvt_ref, gq_ref, wuq_ref, gkv_ref,
                   wuk_ref, wuvt_ref, freq_ref, mask_ref,
                   qm_ref, km_ref, vmt_ref, dq_ref, dk_ref, dvt_ref):
    x = x_ref[0]
    mod = mod_ref[0]
    h = (_rms(x, g1_ref[...], 1e-6) * (1.0 + mod[1:2]) + mod[0:1]).astype(BF16)
    proj = jnp.dot(h, win_ref[...], preferred_element_type=F32)
    dvt_ref[0] = _nt(wdvt_ref[...], h).astype(BF16)

    ang = pos_ref[0].astype(F32) * freq_ref[...]
    cos, sin = jnp.cos(ang), jnp.sin(ang)
    mk = mask_ref[...]
    cos_m = jnp.where(mk[0:1] > 0, cos, 1.0)
    sa_m, sb_m = sin * mk[1:2], sin * mk[2:3]
    cos_d = jnp.where(mk[3:4] > 0, cos, 1.0)
    sa_d, sb_d = sin * mk[4:5], sin * mk[5:6]
    mla_scale = MLA_QK ** -0.5 * LOG2E
    diff_scale = DIFF_DIM ** -0.5 * LOG2E

    cq = _rms(proj[:, C_Q:C_Q + Q_RANK], gq_ref[...], 1e-6)
    q = jnp.dot(cq.astype(BF16), wuq_ref[...], preferred_element_type=F32)
    ckv = _rms(proj[:, C_KV:C_KV + KV_RANK], gkv_ref[...], 1e-6).astype(BF16)
    kn = jnp.dot(ckv, wuk_ref[...], preferred_element_type=F32)
    vmt_ref[0] = _nt(wuvt_ref[...], ckv).astype(BF16)
    kpe = _rope(proj[:, C_KPE:C_KPE + LANES], cos_m, sa_m, sb_m, MLA_HALF)
    cq_m, saq_m, sbq_m = cos_m * mla_scale, sa_m * mla_scale, sb_m * mla_scale
    for hd in range(MLA_HEADS):
        sl = slice(hd * HEAD_PAD, (hd + 1) * HEAD_PAD)
        qm_ref[0, :, sl] = _rope(q[:, sl], cq_m, saq_m, sbq_m, MLA_HALF).astype(BF16)
        km_ref[0, :, sl] = (kn[:, sl] + kpe).astype(BF16)

    cq_d, saq_d, sbq_d = cos_d * diff_scale, sa_d * diff_scale, sb_d * diff_scale
    for hd in range(DIFF_HEADS):
        sl = slice(hd * LANES, (hd + 1) * LANES)
        tq = proj[:, C_DQ + hd * LANES:C_DQ + (hd + 1) * LANES]
        tk = proj[:, C_DK + hd * LANES:C_DK + (hd + 1) * LANES]
        dq_ref[0, :, sl] = _rope(tq, cq_d, saq_d, sbq_d, DIFF_HALF).astype(BF16)
        dk_ref[0, :, sl] = _rope(tk, cos_d, sa_d, sb_d, DIFF_HALF).astype(BF16)


def _rope_tables():
    lane = jnp.arange(LANES)
    f_m = ROPE_THETA ** (-jnp.arange(MLA_HALF, dtype=F32) / MLA_HALF)
    f_d = ROPE_THETA ** (-jnp.arange(DIFF_HALF, dtype=F32) / DIFF_HALF)
    m_x1 = (lane >= MLA_X1) & (lane < MLA_X1 + MLA_HALF)
    m_x2 = (lane >= MLA_X1 + MLA_HALF) & (lane < MLA_X1 + 2 * MLA_HALF)
    d_x1 = (lane % DIFF_DIM) < DIFF_HALF
    d_x2 = ((lane % DIFF_DIM) >= DIFF_HALF) & ((lane % DIFF_DIM) < 2 * DIFF_HALF)
    freq = jnp.where(m_x1 | m_x2, f_m[(lane - MLA_X1) % MLA_HALF], 0.0)
    freq = jnp.where(d_x1 | d_x2, f_d[lane % DIFF_HALF], freq)
    z = jnp.zeros((LANES,), F32)
    masks = jnp.stack([(m_x1 | m_x2).astype(F32), -m_x1.astype(F32), m_x2.astype(F32),
                       (d_x1 | d_x2).astype(F32), -d_x1.astype(F32), d_x2.astype(F32), z, z])
    return freq.reshape(1, LANES).astype(F32), masks


def _pad_in_weights(w_in, w_uq, w_ukv):
    d = w_in.shape[0]
    c_kpe = Q_RANK + KV_RANK
    c_dv = c_kpe + MLA_ROPE + 2 * DIFF_WIDTH
    kpe = jnp.concatenate([jnp.zeros((d, MLA_X1), F32), w_in[:, c_kpe:c_kpe + MLA_ROPE]], axis=1)
    w_in_p = jnp.concatenate([w_in[:, :c_kpe], kpe, w_in[:, c_kpe + MLA_ROPE:c_dv]], axis=1)
    w_dvt = w_in[:, c_dv:].T
    uq = w_uq.reshape(Q_RANK, MLA_HEADS, MLA_QK)
    uq_p = jnp.concatenate([uq[:, :, :MLA_NOPE],
                            jnp.zeros((Q_RANK, MLA_HEADS, MLA_X1 - MLA_NOPE), F32),
                            uq[:, :, MLA_NOPE:]], axis=2).reshape(Q_RANK, MLA_HEADS * HEAD_PAD)
    ukv = w_ukv.reshape(KV_RANK, MLA_HEADS, MLA_NOPE + MLA_V)
    uk_p = jnp.concatenate([ukv[:, :, :MLA_NOPE],
                            jnp.zeros((KV_RANK, MLA_HEADS, HEAD_PAD - MLA_NOPE), F32)],
                           axis=2).reshape(KV_RANK, MLA_HEADS * HEAD_PAD)
    uvt = ukv[:, :, MLA_NOPE:].reshape(KV_RANK, MLA_HEADS * MLA_V).T
    return (w_in_p.astype(BF16), w_dvt.astype(BF16), uq_p.astype(BF16), uk_p.astype(BF16),
            uvt.astype(BF16))


def _in_projection(x, positions, mod, g_norm1, w_in_p, w_dvt, g_q, w_uq_p, g_kv, w_uk_p, w_uvt):
    b, s, d = x.shape
    tm = min(TOKEN_TILE, s)
    freq, masks = _rope_tables()
    full = lambda a: pl.BlockSpec(a.shape, lambda i, j: (0,) * a.ndim)
    tile = lambda w: pl.BlockSpec((1, tm, w), lambda i, j: (i, j, 0))
    tile_t = lambda w: pl.BlockSpec((1, w, tm), lambda i, j: (i, 0, j))
    wide, vm_w = MLA_HEADS * HEAD_PAD, MLA_HEADS * MLA_V
    g1, gq, gkv = g_norm1.reshape(1, d), g_q.reshape(1, Q_RANK), g_kv.reshape(1, KV_RANK)
    tok = lambda w: jax.ShapeDtypeStruct((b, s, w), BF16)
    feat = lambda w: jax.ShapeDtypeStruct((b, w, s), BF16)
    return pl.pallas_call(
        _inproj_kernel,
        grid=(b, s // tm),
        in_specs=[tile(d), tile(1), pl.BlockSpec((1, N_MOD, d), lambda i, j: (i, 0, 0)),
                  full(g1), full(w_in_p), full(w_dvt), full(gq), full(w_uq_p), full(gkv),
                  full(w_uk_p), full(w_uvt), full(freq), full(masks)],
        out_specs=[tile(wide), tile(wide), tile_t(vm_w), tile(DIFF_WIDTH), tile(DIFF_WIDTH),
                   tile_t(DIFF_WIDTH)],
        out_shape=[tok(wide), tok(wide), feat(vm_w), tok(DIFF_WIDTH), tok(DIFF_WIDTH),
                   feat(DIFF_WIDTH)],
        compiler_params=_params(("parallel", "parallel")),
        name="in_projection",
    )(x, positions.reshape(b, s, 1), mod, g1, w_in_p, w_dvt, gq, w_uq_p, gkv, w_uk_p, w_uvt,
      freq, masks)


def _diag_mask(t):
    key = lax.broadcasted_iota(jnp.int32, (t, t), 0) // CHUNK
    qry = lax.broadcasted_iota(jnp.int32, (t, t), 1) // CHUNK
    return key <= qry


def _scores_t(q, k_ref, lanes, q0, t):
    diag = jnp.where(_diag_mask(t), _nt(k_ref[0, q0:q0 + t, lanes], q), NEG_BIG)
    bulk = _nt(k_ref[0, 0:q0, lanes], q) if q0 else None
    return bulk, diag


def _softmax_t(bulk, diag):
    m = jnp.max(diag, axis=0, keepdims=True)
    if bulk is not None:
        m = jnp.maximum(m, jnp.max(bulk, axis=0, keepdims=True))
    p_diag = jnp.exp2(diag - m)
    l = jnp.sum(p_diag, axis=0, keepdims=True)
    p_bulk = None
    if bulk is not None:
        p_bulk = jnp.exp2(bulk - m)
        l = l + jnp.sum(p_bulk, axis=0, keepdims=True)
    return p_bulk, p_diag, l


def _pv_t(vt_ref, rows, p_bulk, p_diag, q0, t):
    o = jnp.dot(vt_ref[0, rows, q0:q0 + t], p_diag.astype(BF16), preferred_element_type=F32)
    if p_bulk is not None:
        o = o + jnp.dot(vt_ref[0, rows, 0:q0], p_bulk.astype(BF16), preferred_element_type=F32)
    return o


def _mla_attn_kernel(q_ref, k_ref, vt_ref, o_ref, *, seq, t):
    for q0 in range(0, seq, t):
        outs = []
        for i in range(2):
            lanes = slice(i * HEAD_PAD, (i + 1) * HEAD_PAD)
            bulk, diag = _scores_t(q_ref[0, q0:q0 + t, lanes], k_ref, lanes, q0, t)
            p_bulk, p_diag, l = _softmax_t(bulk, diag)
            rows = slice(i * MLA_V, (i + 1) * MLA_V)
            outs.append(_pv_t(vt_ref, rows, p_bulk, p_diag, q0, t) / l)
        o_ref[0, q0:q0 + t, :] = jnp.concatenate(outs, axis=0).T.astype(BF16)


def _diff_attn_kernel(lam_ref, gs_ref, q_ref, k_ref, vt_ref, o_ref, *, seq, t):
    lv = lam_ref[...]
    lam = (jnp.exp(jnp.sum(lv[0:1] * lv[1:2], axis=-1, keepdims=True))
           - jnp.exp(jnp.sum(lv[2:3] * lv[3:4], axis=-1, keepdims=True)) + LAMBDA_INIT)
    lane = lax.broadcasted_iota(jnp.int32, (t, 2 * DIFF_DIM), 1)
    every = slice(0, 2 * DIFF_DIM)
    for q0 in range(0, seq, t):
        q = q_ref[0, q0:q0 + t, :]
        zero = jnp.zeros_like(q)
        b1, d1 = _scores_t(jnp.where(lane < DIFF_DIM, q, zero), k_ref, every, q0, t)
        b2, d2 = _scores_t(jnp.where(lane >= DIFF_DIM, q, zero), k_ref, every, q0, t)
        pb1, pd1, l1 = _softmax_t(b1, d1)
        pb2, pd2, l2 = _softmax_t(b2, d2)
        c1, c2 = 1.0 / l1, lam / l2
        w_diag = pd1 * c1 - pd2 * c2
        w_bulk = (pb1 * c1 - pb2 * c2) if q0 else None
        o = _pv_t(vt_ref, every, w_bulk, w_diag, q0, t)
        o = (o * lax.rsqrt(jnp.mean(o * o, axis=0, keepdims=True) + 1e-5) * gs_ref[...]
             * (1.0 - LAMBDA_INIT))
        o_ref[0, q0:q0 + t, :] = o.T.astype(BF16)


def _mla_attention(qm, km, vmt):
    b, s, _ = qm.shape
    t = min(ATTN_TILE, s)
    pair = 2 * HEAD_PAD
    return pl.pallas_call(
        functools.partial(_mla_attn_kernel, seq=s, t=t),
        grid=(b, MLA_HEADS // 2),
        in_specs=[pl.BlockSpec((1, s, pair), lambda i, j: (i, 0, j)),
                  pl.BlockSpec((1, s, pair), lambda i, j: (i, 0, j)),
                  pl.BlockSpec((1, LANES, s), lambda i, j: (i, j, 0))],
        out_specs=pl.BlockSpec((1, s, LANES), lambda i, j: (i, 0, j)),
        out_shape=jax.ShapeDtypeStruct((b, s, MLA_HEADS * MLA_V), BF16),
        compiler_params=_params(("parallel", "parallel")),
        name="mla_attention",
    )(qm, km, vmt)


def _diff_attention(lam_vecs, g_subln, dq, dk, dvt):
    b, s, _ = dq.shape
    t = min(ATTN_TILE, s)
    blk = pl.BlockSpec((1, s, LANES), lambda i, j: (i, 0, j))
    gs = g_subln.reshape(2 * DIFF_DIM, 1)
    return pl.pallas_call(
        functools.partial(_diff_attn_kernel, seq=s, t=t),
        grid=(b, DIFF_HEADS),
        in_specs=[pl.BlockSpec(lam_vecs.shape, lambda i, j: (0, 0)),
                  pl.BlockSpec(gs.shape, lambda i, j: (0, 0)), blk, blk,
                  pl.BlockSpec((1, LANES, s), lambda i, j: (i, j, 0))],
        out_specs=blk,
        out_shape=jax.ShapeDtypeStruct((b, s, DIFF_WIDTH), BF16),
        compiler_params=_params(("parallel", "parallel")),
        name="diff_attention",
    )(lam_vecs, gs, dq, dk, dvt)


def _outproj_kernel(om_ref, od_ref, x_ref, mod_ref, wout_ref, g2_ref, wr_ref, br_ref,
                    x1_ref, h2_ref, idx_ref, tw_ref, rank_ref, cnt_ref, carry_ref):
    first = (pl.program_id(0) == 0) & (pl.program_id(1) == 0)

    @pl.when(first)
    def _():
        carry_ref[...] = jnp.zeros(carry_ref.shape, F32)

    half = om_ref.shape[-1]
    mix = (jnp.dot(om_ref[0], wout_ref[:half, :], preferred_element_type=F32)
           + jnp.dot(od_ref[0], wout_ref[half:, :], preferred_element_type=F32))
    mod = mod_ref[0]
    x1 = x_ref[0] + mod[2:3] * mix
    h2 = _rms(x1, g2_ref[...], 1e-6) * (1.0 + mod[4:5]) + mod[3:4]
    x1_ref[0] = x1
    for c in range(h2_ref.shape[0]):
        h2_ref[c, 0] = h2[:, c * STRIP:(c + 1) * STRIP]

    logits = lax.dot_general(wr_ref[...], h2, (((1,), (1,)), ((), ())),
                             preferred_element_type=F32, precision=HIGHEST) + br_ref[...]
    n_e, t = logits.shape
    e_iota = lax.broadcasted_iota(jnp.int32, (n_e, t), 0)
    vals, idxs = [], []
    rest = logits
    for _ in range(TOP_K):
        m = jnp.max(rest, axis=0, keepdims=True)
        ik = jnp.min(jnp.where(rest == m, e_iota, n_e), axis=0, keepdims=True)
        vals.append(m)
        idxs.append(ik)
        rest = jnp.where(e_iota == ik, -jnp.inf, rest)
    ex = [jnp.exp(v - vals[0]) for v in vals]
    den = ex[0] + ex[1] + ex[2] + ex[3]
    sel = (e_iota == idxs[0]) | (e_iota == idxs[1]) | (e_iota == idxs[2]) | (e_iota == idxs[3])

    before = (lax.broadcasted_iota(jnp.int32, (t, t), 0)
              < lax.broadcasted_iota(jnp.int32, (t, t), 1)).astype(BF16)
    prefix = jnp.dot(sel.astype(BF16), before, preferred_element_type=F32)
    pos = carry_ref[...] + prefix
    for k in range(TOP_K):
        idx_ref[0, k:k + 1, :] = idxs[k]
        tw_ref[0, k:k + 1, :] = ex[k] / den
        rk = jnp.sum(jnp.where(e_iota == idxs[k], pos, 0.0), axis=0, keepdims=True)
        rank_ref[0, k:k + 1, :] = rk.astype(jnp.int32)
    total = carry_ref[...] + jnp.sum(sel.astype(F32), axis=1, keepdims=True)
    carry_ref[...] = total
    cnt_ref[...] = total.astype(jnp.int32)


def _out_projection(o_mla, o_diff, x, mod, w_out, g_norm2, w_router, b_router):
    b, s, d = x.shape
    tm = min(TOKEN_TILE, s)
    nt = s // tm
    half = o_mla.shape[-1]
    full = lambda a: pl.BlockSpec(a.shape, lambda i, j: (0,) * a.ndim)
    tile = lambda w: pl.BlockSpec((1, tm, w), lambda i, j: (i, j, 0))
    route = pl.BlockSpec((1, TOP_K, tm), lambda i, j: (i * nt + j, 0, 0))
    g2 = g_norm2.reshape(1, d)
    wr = w_router.T
    br = b_router.reshape(N_EXPERTS, 1)
    w_out_b = w_out.astype(BF16)
    return pl.pallas_call(
        _outproj_kernel,
        grid=(b, nt),
        in_specs=[tile(half), tile(half), tile(d), pl.BlockSpec((1, N_MOD, d), lambda i, j: (i, 0, 0)),
                  full(w_out_b), full(g2), full(wr), full(br)],
        out_specs=[tile(d), pl.BlockSpec((d // STRIP, 1, tm, STRIP), lambda i, j: (0, i, j, 0)),
                   route, route, route,
                   pl.BlockSpec((N_EXPERTS, 1), lambda i, j: (0, 0))],
        out_shape=[jax.ShapeDtypeStruct((b, s, d), F32),
                   jax.ShapeDtypeStruct((d // STRIP, b, s, STRIP), F32),
                   jax.ShapeDtypeStruct((b * nt, TOP_K, tm), jnp.int32),
                   jax.ShapeDtypeStruct((b * nt, TOP_K, tm), F32),
                   jax.ShapeDtypeStruct((b * nt, TOP_K, tm), jnp.int32),
                   jax.ShapeDtypeStruct((N_EXPERTS, 1), jnp.int32)],
        scratch_shapes=[pltpu.VMEM((N_EXPERTS, 1), F32)],
        compiler_params=_params(("arbitrary", "arbitrary")),
        name="out_projection_router",
    )(o_mla, o_diff, x, mod, w_out_b, g2, wr, br)


def _split_gate_up_perm():
    r = lax.broadcasted_iota(jnp.int32, (MXU_COLS, MXU_COLS), 0)
    c = lax.broadcasted_iota(jnp.int32, (MXU_COLS, MXU_COLS), 1)
    src = jnp.where(c < LANES, 2 * c, 2 * (c - LANES) + 1)
    return (r == src).astype(BF16)


def _expert_kernel(ie_ref, ib_ref, lo_ref, hi_ref, nv_ref, xs_ref, wgu_ref, bgu_ref, wd_ref, bd_ref,
                   y_ref, wgu_s, wd_s, y_s):
    i = pl.program_id(0)
    prev = jnp.maximum(i - 1, 0)
    valid = i < nv_ref[0]
    new_expert = (i == 0) | (ie_ref[i] != ie_ref[prev])
    new_block = (i == 0) | (ib_ref[i] != ib_ref[prev])
    n_groups = wgu_s.shape[1] // MXU_COLS
    n_strips = y_ref.shape[0]
    row = lax.broadcasted_iota(jnp.int32, (y_ref.shape[1], 1), 0)
    mine = (row >= lo_ref[i]) & (row < hi_ref[i])

    @pl.when(valid & new_expert)
    def _():
        perm = _split_gate_up_perm()
        for c in range(n_groups):
            cols = slice(c * MXU_COLS, (c + 1) * MXU_COLS)
            w = wgu_ref[0, :, cols].astype(BF16)
            wgu_s[:, cols] = jnp.dot(w, perm, preferred_element_type=F32).astype(BF16)
        wd_s[...] = wd_ref[0].astype(BF16)

    @pl.when(valid)
    def _():
        xb = jnp.concatenate([xs_ref[c].astype(BF16) for c in range(n_strips)], axis=1)
        gu = jnp.dot(xb, wgu_s[...], preferred_element_type=F32) + bgu_ref[0]
        acts = []
        for c in range(n_groups):
            g = jnp.minimum(gu[:, c * MXU_COLS:c * MXU_COLS + LANES], SWIGLU_LIMIT)
            u = jnp.clip(gu[:, c * MXU_COLS + LANES:(c + 1) * MXU_COLS], -SWIGLU_LIMIT, SWIGLU_LIMIT)
            acts.append((g / (1.0 + jnp.exp(-SWIGLU_ALPHA * g)) * (u + 1.0)).astype(BF16))
        act = jnp.concatenate(acts, axis=1)
        y_s[...] = jnp.dot(act, wd_s[...], preferred_element_type=F32) + bd_ref[0]

    @pl.when(valid & new_block)
    def _():
        for c in range(n_strips):
            y_ref[c] = jnp.where(mine, y_s[:, c * STRIP:(c + 1) * STRIP], 0.0)

    @pl.when(valid & jnp.logical_not(new_block))
    def _():
        for c in range(n_strips):
            y_ref[c] = jnp.where(mine, y_s[:, c * STRIP:(c + 1) * STRIP], y_ref[c])


def _expert_ffn(items, xs, w_gate_up, b_gate_up_grouped, w_down, b_down):
    n_strips, a, _ = xs.shape
    d = n_strips * STRIP
    dff2 = w_gate_up.shape[-1]
    dff = dff2 // 2
    n_items = items[0].shape[0]
    wmap = lambda i, ie, ib, lo, hi, nv: (ie[i], 0, 0)
    rmap = lambda i, ie, ib, lo, hi, nv: (0, ib[i], 0)
    grid_spec = pltpu.PrefetchScalarGridSpec(
        num_scalar_prefetch=5,
        grid=(n_items,),
        in_specs=[pl.BlockSpec((n_strips, MOE_ROWS, STRIP), rmap),
                  pl.BlockSpec((1, d, dff2), wmap), pl.BlockSpec((1, 1, dff2), wmap),
                  pl.BlockSpec((1, dff, d), wmap), pl.BlockSpec((1, 1, d), wmap)],
        out_specs=pl.BlockSpec((n_strips, MOE_ROWS, STRIP), rmap),
        scratch_shapes=[pltpu.VMEM((d, dff2), BF16), pltpu.VMEM((dff, d), BF16),
                        pltpu.VMEM((MOE_ROWS, d), F32)],
    )
    return pl.pallas_call(
        _expert_kernel,
        grid_spec=grid_spec,
        out_shape=jax.ShapeDtypeStruct((n_strips, a, STRIP), F32),
        compiler_params=_params(("arbitrary",), EXPERT_VMEM_LIMIT),
        name="expert_ffn",
    )(*items, xs, w_gate_up, b_gate_up_grouped, w_down, b_down)


def _expert_work_items(counts, n_rows):
    e_ids = jnp.arange(N_EXPERTS, dtype=jnp.int32)
    ends = jnp.cumsum(counts).astype(jnp.int32)
    starts = ends - counts
    first_blk = starts // MOE_ROWS
    last_blk = (ends - 1) // MOE_ROWS
    per_expert = jnp.where(counts > 0, last_blk - first_blk + 1, 0)
    item_end = jnp.cumsum(per_expert).astype(jnp.int32)
    item_start = item_end - per_expert
    total = item_end[-1:]
    max_items = n_rows // MOE_ROWS + N_EXPERTS - 1
    it = jnp.clip(jnp.arange(max_items, dtype=jnp.int32), 0, jnp.maximum(total - 1, 0))
    expert = jnp.sum(item_end[None, :] <= it[:, None], axis=1).astype(jnp.int32)
    pick = lambda v: jnp.sum(jnp.where(expert[:, None] == e_ids, v, 0), axis=1)
    block = pick(first_blk) + it - pick(item_start)
    lo = jnp.maximum(pick(starts) - block * MOE_ROWS, 0)
    hi = jnp.minimum(pick(ends) - block * MOE_ROWS, MOE_ROWS)
    return expert, block, lo, hi, total, starts


def _gather_strips(strips, idx):
    c, r, w = strips.shape
    offsets = jnp.arange(c, dtype=jnp.int32)[:, None] * r
    out = _gather_rows(strips.reshape(c * r, w), (idx[None, :] + offsets).reshape(-1))
    return out.reshape(c, idx.shape[0], w)


def _gather_rows(table, idx):
    m = idx.shape[0]
    w = table.shape[1]
    window = GATHER_WINDOW
    mesh = plsc.VectorSubcoreMesh(core_axis_name="core", subcore_axis_name="subcore")

    @pl.kernel(out_type=jax.ShapeDtypeStruct((m, w), table.dtype), mesh=mesh)
    def gather_kernel(table_hbm, idx_hbm, out_hbm):
        def body(idx_vmem, out_vmem):
            pltpu.sync_copy(table_hbm.at[idx_vmem.at[0]], out_vmem)

        pltpu.emit_pipeline(
            body,
            grid=(m // window,),
            in_specs=[pl.BlockSpec((1, window), index_map=lambda i: (0, i))],
            out_specs=[pl.BlockSpec((window, w), index_map=lambda i: (i, 0))],
            core_axis_name=("core", "subcore"),
            dimension_semantics=(pltpu.PARALLEL,),
        )(idx_hbm, out_hbm)

    return gather_kernel(table, idx.reshape(1, m))


def _combine_kernel(yk_ref, tw_ref, x1_ref, mod_ref, gf_ref, o_ref):
    tw = tw_ref[0]
    strips = []
    for c in range(yk_ref.shape[0]):
        yc = yk_ref[c, 0, 0] * tw[:, 0:1]
        for k in range(1, TOP_K):
            yc = yc + yk_ref[c, k, 0] * tw[:, k:k + 1]
        strips.append(yc)
    y = jnp.concatenate(strips, axis=1)
    x2 = x1_ref[0] + mod_ref[0][5:6] * y
    o_ref[0] = _rms(x2, gf_ref[...], 1e-6)


def _combine(yk, tw, x1, mod, g_final):
    b, s, d = x1.shape
    tm = min(TOKEN_TILE, s)
    gf = g_final.reshape(1, d)
    return pl.pallas_call(
        _combine_kernel,
        grid=(b, s // tm),
        in_specs=[pl.BlockSpec((d // STRIP, TOP_K, 1, tm, STRIP), lambda i, j: (0, 0, i, j, 0)),
                  pl.BlockSpec((1, tm, TOP_K), lambda i, j: (i, j, 0)),
                  pl.BlockSpec((1, tm, d), lambda i, j: (i, j, 0)),
                  pl.BlockSpec((1, N_MOD, d), lambda i, j: (i, 0, 0)),
                  pl.BlockSpec((1, d), lambda i, j: (0, 0))],
        out_specs=pl.BlockSpec((1, tm, d), lambda i, j: (i, j, 0)),
        out_shape=jax.ShapeDtypeStruct((b, s, d), F32),
        compiler_params=_params(("parallel", "parallel")),
        name="combine_final_norm",
    )(yk, tw, x1, mod, gf)


def kernel(x, c, positions, w_ada, b_ada, g_norm1, w_in, g_q_norm, w_uq, g_kv_norm, w_ukv,
           lambda_q1, lambda_k1, lambda_q2, lambda_k2, g_subln, w_out, g_norm2,
           w_router, b_router, w_gate_up, b_gate_up, w_down, b_down, g_final):
    b, s, d = x.shape
    n = b * s
    l = 0
    mod = _ada_mod(c, w_ada[l], b_ada[l]).reshape(b, N_MOD, d)

    w_in_p, w_dvt, w_uq_p, w_uk_p, w_uvt = _pad_in_weights(w_in[l], w_uq[l], w_ukv[l])
    qm, km, vmt, dq, dk, dvt = _in_projection(x, positions, mod, g_norm1[l], w_in_p, w_dvt,
                                              g_q_norm[l], w_uq_p, g_kv_norm[l], w_uk_p, w_uvt)
    o_mla = _mla_attention(qm, km, vmt)
    lam_vecs = jnp.stack([lambda_q1[l], lambda_k1[l], lambda_q2[l], lambda_k2[l]]).astype(F32)
    o_diff = _diff_attention(lam_vecs, g_subln[l], dq, dk, dvt)

    x1, h2, idx, tw, rank, cnt = _out_projection(o_mla, o_diff, x, mod, w_out[l], g_norm2[l],
                                                 w_router[l], b_router[l])

    to_kn = lambda a: a.transpose(1, 0, 2).reshape(TOP_K, n)
    idx, tw, rank = to_kn(idx), to_kn(tw), to_kn(rank)
    n_rows = n * TOP_K
    *items, starts = _expert_work_items(cnt[:, 0], n_rows)
    e_ids = jnp.arange(N_EXPERTS, dtype=jnp.int32)
    slot = rank + jnp.sum(jnp.where(idx[..., None] == e_ids, starts, 0), axis=-1)
    tok = jnp.arange(n, dtype=jnp.int32)
    row_tok = jnp.sort((idx * n + tok).reshape(-1)) % n

    xs = _gather_strips(h2.reshape(d // STRIP, n, STRIP), row_tok)
    bgu = (b_gate_up[l].reshape(N_EXPERTS, EXPERT_DFF // LANES, LANES, 2)
           .transpose(0, 1, 3, 2).reshape(N_EXPERTS, 1, 2 * EXPERT_DFF))
    y = _expert_ffn(items, xs, w_gate_up[l], bgu, w_down[l], b_down[l].reshape(N_EXPERTS, 1, d))

    yk = _gather_strips(y, slot.reshape(-1)).reshape(d // STRIP, TOP_K, b, s, STRIP)
    return _combine(yk, tw.T.reshape(b, s, TOP_K), x1, mod, g_final)
```

```python
import functools
import math

import jax
import jax.numpy as jnp
from jax import lax
from jax.experimental import pallas as pl
from jax.experimental.pallas import tpu as pltpu
from jax.experimental.pallas import tpu_sc as plsc

F32 = jnp.float32
BF16 = jnp.bfloat16
HIGHEST = lax.Precision.HIGHEST

D_MODEL = 1024
CHUNK = 64
ROPE_THETA = 500000.0
MLA_HEADS = 8
MLA_NOPE = 64
MLA_ROPE = 32
MLA_V = 64
MLA_QK = MLA_NOPE + MLA_ROPE
Q_RANK = 384
KV_RANK = 256
DIFF_HEADS = 4
DIFF_DIM = 64
DIFF_ROPE = DIFF_DIM // 4
DIFF_WIDTH = DIFF_HEADS * 2 * DIFF_DIM
N_EXPERTS = 32
TOP_K = 4
EXPERT_DFF = D_MODEL
SWIGLU_LIMIT = 7.0
SWIGLU_ALPHA = 1.702
N_MOD = 6
LAMBDA_INIT = 0.8 - 0.6 * math.exp(-0.3 * 0)
LOG2E = math.log2(math.e)

LANES = 128
MXU_COLS = 256
HEAD_PAD = LANES
TOKEN_TILE = 512
ATTN_TILE = 256
MOE_ROWS = 256
GATHER_WINDOW = 128
STRIP = 256
VMEM_LIMIT = 48 * 1024 * 1024
EXPERT_VMEM_LIMIT = 58 * 1024 * 1024
NEG_BIG = -1e30

MLA_X1 = 96
MLA_HALF = MLA_ROPE // 2
DIFF_HALF = DIFF_ROPE // 2

C_Q = 0
C_KV = Q_RANK
C_KPE = Q_RANK + KV_RANK
C_DQ = C_KPE + LANES
C_DK = C_DQ + DIFF_WIDTH
IN_COLS_PAD = C_DK + DIFF_WIDTH


def _rms(x, g, eps):
    return x * lax.rsqrt(jnp.mean(x * x, axis=-1, keepdims=True) + eps) * g


def _pack_pairs(x):
    w = x.shape[1] // 2
    as_bits = lambda v: lax.bitcast_convert_type(v.astype(BF16).astype(F32), jnp.uint32)
    return (as_bits(x[:, w:]) & jnp.uint32(0xFFFF0000)) | (as_bits(x[:, :w]) >> 16)


def _unpack_pairs(p):
    lo = lax.bitcast_convert_type(p << 16, F32)
    hi = lax.bitcast_convert_type(p & jnp.uint32(0xFFFF0000), F32)
    return lo, hi


def _params(sem, limit=VMEM_LIMIT):
    return pltpu.CompilerParams(dimension_semantics=sem, vmem_limit_bytes=limit)


def _nt(a, b):
    return lax.dot_general(a, b, (((1,), (1,)), ((), ())), preferred_element_type=F32)


def _ada_kernel(c_ref, w_ref, b_ref, o_ref):
    c = c_ref[...]
    sc = c / (1.0 + jnp.exp(-c))
    o_ref[...] = jnp.dot(sc, w_ref[...], preferred_element_type=F32, precision=HIGHEST) + b_ref[...]


def _ada_mod(c, w_ada, b_ada):
    b, d = c.shape
    n = w_ada.shape[1]
    return pl.pallas_call(
        _ada_kernel,
        grid=(n // d,),
        in_specs=[pl.BlockSpec((b, d), lambda j: (0, 0)),
                  pl.BlockSpec((d, d), lambda j: (0, j)),
                  pl.BlockSpec((1, d), lambda j: (0, j))],
        out_specs=pl.BlockSpec((b, d), lambda j: (0, j)),
        out_shape=jax.ShapeDtypeStruct((b, n), F32),
        compiler_params=_params(("arbitrary",)),
        name="ada_mod",
    )(c, w_ada, b_ada.reshape(1, n))


def _rope(t, cos, sin_a, sin_b, shift):
    return (t * cos + pltpu.roll(t, LANES - shift, 1) * sin_a + pltpu.roll(t, shift, 1) * sin_b)


def _inproj_kernel(x_ref, pos_ref, mod_ref, g1_ref, win_ref, wdvt_ref, gq_ref, wuq_ref, gkv_ref,
                   wuk_ref, wuvt_ref, freq_ref, mask_ref,
                   qm_ref, km_ref, vmt_ref, dq_ref, dk_ref, dvt_ref):
    x = x_ref[0]
    mod = mod_ref[0]
    h = (_rms(x, g1_ref[...], 1e-6) * (1.0 + mod[1:2]) + mod[0:1]).astype(BF16)
    proj = jnp.dot(h, win_ref[...], preferred_element_type=F32)
    dvt_ref[0] = _nt(wdvt_ref[...], h).astype(BF16)

    ang = pos_ref[0].astype(F32) * freq_ref[...]
    cos, sin = jnp.cos(ang), jnp.sin(ang)
    mk = mask_ref[...]
    cos_m = jnp.where(mk[0:1] > 0, cos, 1.0)
    sa_m, sb_m = sin * mk[1:2], sin * mk[2:3]
    cos_d = jnp.where(mk[3:4] > 0, cos, 1.0)
    sa_d, sb_d = sin * mk[4:5], sin * mk[5:6]
    mla_scale = MLA_QK ** -0.5 * LOG2E
    diff_scale = DIFF_DIM ** -0.5 * LOG2E

    cq = _rms(proj[:, C_Q:C_Q + Q_RANK], gq_ref[...], 1e-6)
    q = jnp.dot(cq.astype(BF16), wuq_ref[...], preferred_element_type=F32)
    ckv = _rms(proj[:, C_KV:C_KV + KV_RANK], gkv_ref[...], 1e-6).astype(BF16)
    kn = jnp.dot(ckv, wuk_ref[...], preferred_element_type=F32)
    vmt_ref[0] = _nt(wuvt_ref[...], ckv).astype(BF16)
    kpe = _rope(proj[:, C_KPE:C_KPE + LANES], cos_m, sa_m, sb_m, MLA_HALF)
    cq_m, saq_m, sbq_m = cos_m * mla_scale, sa_m * mla_scale, sb_m * mla_scale
    for hd in range(MLA_HEADS):
        sl = slice(hd * HEAD_PAD, (hd + 1) * HEAD_PAD)
        qm_ref[0, :, sl] = _rope(q[:, sl], cq_m, saq_m, sbq_m, MLA_HALF).astype(BF16)
        km_ref[0, :, sl] = (kn[:, sl] + kpe).astype(BF16)

    cq_d, saq_d, sbq_d = cos_d * diff_scale, sa_d * diff_scale, sb_d * diff_scale
    for hd in range(DIFF_HEADS):
        sl = slice(hd * LANES, (hd + 1) * LANES)
        tq = proj[:, C_DQ + hd * LANES:C_DQ + (hd + 1) * LANES]
        tk = proj[:, C_DK + hd * LANES:C_DK + (hd + 1) * LANES]
        dq_ref[0, :, sl] = _rope(tq, cq_d, saq_d, sbq_d, DIFF_HALF).astype(BF16)
        dk_ref[0, :, sl] = _rope(tk, cos_d, sa_d, sb_d, DIFF_HALF).astype(BF16)


def _rope_tables():
    lane = jnp.arange(LANES)
    f_m = ROPE_THETA ** (-jnp.arange(MLA_HALF, dtype=F32) / MLA_HALF)
    f_d = ROPE_THETA ** (-jnp.arange(DIFF_HALF, dtype=F32) / DIFF_HALF)
    m_x1 = (lane >= MLA_X1) & (lane < MLA_X1 + MLA_HALF)
    m_x2 = (lane >= MLA_X1 + MLA_HALF) & (lane < MLA_X1 + 2 * MLA_HALF)
    d_x1 = (lane % DIFF_DIM) < DIFF_HALF
    d_x2 = ((lane % DIFF_DIM) >= DIFF_HALF) & ((lane % DIFF_DIM) < 2 * DIFF_HALF)
    freq = jnp.where(m_x1 | m_x2, f_m[(lane - MLA_X1) % MLA_HALF], 0.0)
    freq = jnp.where(d_x1 | d_x2, f_d[lane % DIFF_HALF], freq)
    z = jnp.zeros((LANES,), F32)
    masks = jnp.stack([(m_x1 | m_x2).astype(F32), -m_x1.astype(F32), m_x2.astype(F32),
                       (d_x1 | d_x2).astype(F32), -d_x1.astype(F32), d_x2.astype(F32), z, z])
    return freq.reshape(1, LANES).astype(F32), masks


def _pad_in_weights(w_in, w_uq, w_ukv):
    d = w_in.shape[0]
    c_kpe = Q_RANK + KV_RANK
    c_dv = c_kpe + MLA_ROPE + 2 * DIFF_WIDTH
    kpe = jnp.concatenate([jnp.zeros((d, MLA_X1), F32), w_in[:, c_kpe:c_kpe + MLA_ROPE]], axis=1)
    w_in_p = jnp.concatenate([w_in[:, :c_kpe], kpe, w_in[:, c_kpe + MLA_ROPE:c_dv]], axis=1)
    w_dvt = w_in[:, c_dv:].T
    uq = w_uq.reshape(Q_RANK, MLA_HEADS, MLA_QK)
    uq_p = jnp.concatenate([uq[:, :, :MLA_NOPE],
                            jnp.zeros((Q_RANK, MLA_HEADS, MLA_X1 - MLA_NOPE), F32),
                            uq[:, :, MLA_NOPE:]], axis=2).reshape(Q_RANK, MLA_HEADS * HEAD_PAD)
    ukv = w_ukv.reshape(KV_RANK, MLA_HEADS, MLA_NOPE + MLA_V)
    uk_p = jnp.concatenate([ukv[:, :, :MLA_NOPE],
                            jnp.zeros((KV_RANK, MLA_HEADS, HEAD_PAD - MLA_NOPE), F32)],
                           axis=2).reshape(KV_RANK, MLA_HEADS * HEAD_PAD)
    uvt = ukv[:, :, MLA_NOPE:].reshape(KV_RANK, MLA_HEADS * MLA_V).T
    return (w_in_p.astype(BF16), w_dvt.astype(BF16), uq_p.astype(BF16), uk_p.astype(BF16),
            uvt.astype(BF16))


def _in_projection(x, positions, mod, g_norm1, w_in_p, w_dvt, g_q, w_uq_p, g_kv, w_uk_p, w_uvt):
    b, s, d = x.shape
    tm = min(TOKEN_TILE, s)
    freq, masks = _rope_tables()
    full = lambda a: pl.BlockSpec(a.shape, lambda i, j: (0,) * a.ndim)
    tile = lambda w: pl.BlockSpec((1, tm, w), lambda i, j: (i, j, 0))
    tile_t = lambda w: pl.BlockSpec((1, w, tm), lambda i, j: (i, 0, j))
    wide, vm_w = MLA_HEADS * HEAD_PAD, MLA_HEADS * MLA_V
    g1, gq, gkv = g_norm1.reshape(1, d), g_q.reshape(1, Q_RANK), g_kv.reshape(1, KV_RANK)
    tok = lambda w: jax.ShapeDtypeStruct((b, s, w), BF16)
    feat = lambda w: jax.ShapeDtypeStruct((b, w, s), BF16)
    return pl.pallas_call(
        _inproj_kernel,
        grid=(b, s // tm),
        in_specs=[tile(d), tile(1), pl.BlockSpec((1, N_MOD, d), lambda i, j: (i, 0, 0)),
                  full(g1), full(w_in_p), full(w_dvt), full(gq), full(w_uq_p), full(gkv),
                  full(w_uk_p), full(w_uvt), full(freq), full(masks)],
        out_specs=[tile(wide), tile(wide), tile_t(vm_w), tile(DIFF_WIDTH), tile(DIFF_WIDTH),
                   tile_t(DIFF_WIDTH)],
        out_shape=[tok(wide), tok(wide), feat(vm_w), tok(DIFF_WIDTH), tok(DIFF_WIDTH),
                   feat(DIFF_WIDTH)],
        compiler_params=_params(("parallel", "parallel")),
        name="in_projection",
    )(x, positions.reshape(b, s, 1), mod, g1, w_in_p, w_dvt, gq, w_uq_p, gkv, w_uk_p, w_uvt,
      freq, masks)


def _diag_mask(t):
    key = lax.broadcasted_iota(jnp.int32, (t, t), 0) // CHUNK
    qry = lax.broadcasted_iota(jnp.int32, (t, t), 1) // CHUNK
    return key <= qry


def _scores_t(q, k_ref, lanes, q0, t):
    diag = jnp.where(_diag_mask(t), _nt(k_ref[0, q0:q0 + t, lanes], q), NEG_BIG)
    bulk = _nt(k_ref[0, 0:q0, lanes], q) if q0 else None
    return bulk, diag


def _softmax_t(bulk, diag):
    m = jnp.max(diag, axis=0, keepdims=True)
    if bulk is not None:
        m = jnp.maximum(m, jnp.max(bulk, axis=0, keepdims=True))
    p_diag = jnp.exp2(diag - m)
    l = jnp.sum(p_diag, axis=0, keepdims=True)
    p_bulk = None
    if bulk is not None:
        p_bulk = jnp.exp2(bulk - m)
        l = l + jnp.sum(p_bulk, axis=0, keepdims=True)
    return p_bulk, p_diag, l


def _pv_t(vt_ref, rows, p_bulk, p_diag, q0, t):
    o = jnp.dot(vt_ref[0, rows, q0:q0 + t], p_diag.astype(BF16), preferred_element_type=F32)
    if p_bulk is not None:
        o = o + jnp.dot(vt_ref[0, rows, 0:q0], p_bulk.astype(BF16), preferred_element_type=F32)
    return o


def _mla_attn_kernel(q_ref, k_ref, vt_ref, o_ref, *, seq, t):
    for q0 in range(0, seq, t):
        outs = []
        for i in range(2):
            lanes = slice(i * HEAD_PAD, (i + 1) * HEAD_PAD)
            bulk, diag = _scores_t(q_ref[0, q0:q0 + t, lanes], k_ref, lanes, q0, t)
            p_bulk, p_diag, l = _softmax_t(bulk, diag)
            rows = slice(i * MLA_V, (i + 1) * MLA_V)
            outs.append(_pv_t(vt_ref, rows, p_bulk, p_diag, q0, t) / l)
        o_ref[0, q0:q0 + t, :] = jnp.concatenate(outs, axis=0).T.astype(BF16)


def _diff_attn_kernel(lam_ref, gs_ref, q_ref, k_ref, vt_ref, o_ref, *, seq, t):
    lv = lam_ref[...]
    lam = (jnp.exp(jnp.sum(lv[0:1] * lv[1:2], axis=-1, keepdims=True))
           - jnp.exp(jnp.sum(lv[2:3] * lv[3:4], axis=-1, keepdims=True)) + LAMBDA_INIT)
    lane = lax.broadcasted_iota(jnp.int32, (t, 2 * DIFF_DIM), 1)
    every = slice(0, 2 * DIFF_DIM)
    for q0 in range(0, seq, t):
        q = q_ref[0, q0:q0 + t, :]
        zero = jnp.zeros_like(q)
        b1, d1 = _scores_t(jnp.where(lane < DIFF_DIM, q, zero), k_ref, every, q0, t)
        b2, d2 = _scores_t(jnp.where(lane >= DIFF_DIM, q, zero), k_ref, every, q0, t)
        pb1, pd1, l1 = _softmax_t(b1, d1)
        pb2, pd2, l2 = _softmax_t(b2, d2)
        c1, c2 = 1.0 / l1, lam / l2
        w_diag = pd1 * c1 - pd2 * c2
        w_bulk = (pb1 * c1 - pb2 * c2) if q0 else None
        o = _pv_t(vt_ref, every, w_bulk, w_diag, q0, t)
        o = (o * lax.rsqrt(jnp.mean(o * o, axis=0, keepdims=True) + 1e-5) * gs_ref[...]
             * (1.0 - LAMBDA_INIT))
        o_ref[0, q0:q0 + t, :] = o.T.astype(BF16)


def _mla_attention(qm, km, vmt):
    b, s, _ = qm.shape
    t = min(ATTN_TILE, s)
    pair = 2 * HEAD_PAD
    return pl.pallas_call(
        functools.partial(_mla_attn_kernel, seq=s, t=t),
        grid=(b, MLA_HEADS // 2),
        in_specs=[pl.BlockSpec((1, s, pair), lambda i, j: (i, 0, j)),
                  pl.BlockSpec((1, s, pair), lambda i, j: (i, 0, j)),
                  pl.BlockSpec((1, LANES, s), lambda i, j: (i, j, 0))],
        out_specs=pl.BlockSpec((1, s, LANES), lambda i, j: (i, 0, j)),
        out_shape=jax.ShapeDtypeStruct((b, s, MLA_HEADS * MLA_V), BF16),
        compiler_params=_params(("parallel", "parallel")),
        name="mla_attention",
    )(qm, km, vmt)


def _diff_attention(lam_vecs, g_subln, dq, dk, dvt):
    b, s, _ = dq.shape
    t = min(ATTN_TILE, s)
    blk = pl.BlockSpec((1, s, LANES), lambda i, j: (i, 0, j))
    gs = g_subln.reshape(2 * DIFF_DIM, 1)
    return pl.pallas_call(
        functools.partial(_diff_attn_kernel, seq=s, t=t),
        grid=(b, DIFF_HEADS),
        in_specs=[pl.BlockSpec(lam_vecs.shape, lambda i, j: (0, 0)),
                  pl.BlockSpec(gs.shape, lambda i, j: (0, 0)), blk, blk,
                  pl.BlockSpec((1, LANES, s), lambda i, j: (i, j, 0))],
        out_specs=blk,
        out_shape=jax.ShapeDtypeStruct((b, s, DIFF_WIDTH), BF16),
        compiler_params=_params(("parallel", "parallel")),
        name="diff_attention",
    )(lam_vecs, gs, dq, dk, dvt)


def _outproj_kernel(om_ref, od_ref, x_ref, mod_ref, wout_ref, g2_ref, wr_ref, br_ref,
                    x1_ref, h2_ref, idx_ref, tw_ref, rank_ref, cnt_ref, carry_ref):
    first = (pl.program_id(0) == 0) & (pl.program_id(1) == 0)

    @pl.when(first)
    def _():
        carry_ref[...] = jnp.zeros(carry_ref.shape, F32)

    half = om_ref.shape[-1]
    mix = (jnp.dot(om_ref[0], wout_ref[:half, :], preferred_element_type=F32)
           + jnp.dot(od_ref[0], wout_ref[half:, :], preferred_element_type=F32))
    mod = mod_ref[0]
    x1 = x_ref[0] + mod[2:3] * mix
    h2 = _rms(x1, g2_ref[...], 1e-6) * (1.0 + mod[4:5]) + mod[3:4]
    x1_ref[0] = x1
    packed = _pack_pairs(h2)
    for c in range(h2_ref.shape[0]):
        h2_ref[c, 0] = packed[:, c * STRIP:(c + 1) * STRIP]

    logits = lax.dot_general(wr_ref[...], h2, (((1,), (1,)), ((), ())),
                             preferred_element_type=F32, precision=HIGHEST) + br_ref[...]
    n_e, t = logits.shape
    e_iota = lax.broadcasted_iota(jnp.int32, (n_e, t), 0)
    vals, idxs = [], []
    rest = logits
    for _ in range(TOP_K):
        m = jnp.max(rest, axis=0, keepdims=True)
        ik = jnp.min(jnp.where(rest == m, e_iota, n_e), axis=0, keepdims=True)
        vals.append(m)
        idxs.append(ik)
        rest = jnp.where(e_iota == ik, -jnp.inf, rest)
    ex = [jnp.exp(v - vals[0]) for v in vals]
    den = ex[0] + ex[1] + ex[2] + ex[3]
    sel = (e_iota == idxs[0]) | (e_iota == idxs[1]) | (e_iota == idxs[2]) | (e_iota == idxs[3])

    before = (lax.broadcasted_iota(jnp.int32, (t, t), 0)
              < lax.broadcasted_iota(jnp.int32, (t, t), 1)).astype(BF16)
    prefix = jnp.dot(sel.astype(BF16), before, preferred_element_type=F32)
    pos = carry_ref[...] + prefix
    for k in range(TOP_K):
        idx_ref[0, k:k + 1, :] = idxs[k]
        tw_ref[0, k:k + 1, :] = ex[k] / den
        rk = jnp.sum(jnp.where(e_iota == idxs[k], pos, 0.0), axis=0, keepdims=True)
        rank_ref[0, k:k + 1, :] = rk.astype(jnp.int32)
    total = carry_ref[...] + jnp.sum(sel.astype(F32), axis=1, keepdims=True)
    carry_ref[...] = total
    cnt_ref[...] = total.astype(jnp.int32)


def _out_projection(o_mla, o_diff, x, mod, w_out, g_norm2, w_router, b_router):
    b, s, d = x.shape
    tm = min(TOKEN_TILE, s)
    nt = s // tm
    half = o_mla.shape[-1]
    full = lambda a: pl.BlockSpec(a.shape, lambda i, j: (0,) * a.ndim)
    tile = lambda w: pl.BlockSpec((1, tm, w), lambda i, j: (i, j, 0))
    route = pl.BlockSpec((1, TOP_K, tm), lambda i, j: (i * nt + j, 0, 0))
    g2 = g_norm2.reshape(1, d)
    wr = w_router.T
    br = b_router.reshape(N_EXPERTS, 1)
    w_out_b = w_out.astype(BF16)
    n_strips = d // (2 * STRIP)
    return pl.pallas_call(
        _outproj_kernel,
        grid=(b, nt),
        in_specs=[tile(half), tile(half), tile(d), pl.BlockSpec((1, N_MOD, d), lambda i, j: (i, 0, 0)),
                  full(w_out_b), full(g2), full(wr), full(br)],
        out_specs=[tile(d), pl.BlockSpec((n_strips, 1, tm, STRIP), lambda i, j: (0, i, j, 0)),
                   route, route, route,
                   pl.BlockSpec((N_EXPERTS, 1), lambda i, j: (0, 0))],
        out_shape=[jax.ShapeDtypeStruct((b, s, d), F32),
                   jax.ShapeDtypeStruct((n_strips, b, s, STRIP), jnp.uint32),
                   jax.ShapeDtypeStruct((b * nt, TOP_K, tm), jnp.int32),
                   jax.ShapeDtypeStruct((b * nt, TOP_K, tm), F32),
                   jax.ShapeDtypeStruct((b * nt, TOP_K, tm), jnp.int32),
                   jax.ShapeDtypeStruct((N_EXPERTS, 1), jnp.int32)],
        scratch_shapes=[pltpu.VMEM((N_EXPERTS, 1), F32)],
        compiler_params=_params(("arbitrary", "arbitrary")),
        name="out_projection_router",
    )(o_mla, o_diff, x, mod, w_out_b, g2, wr, br)


def _split_gate_up_perm():
    r = lax.broadcasted_iota(jnp.int32, (MXU_COLS, MXU_COLS), 0)
    c = lax.broadcasted_iota(jnp.int32, (MXU_COLS, MXU_COLS), 1)
    src = jnp.where(c < LANES, 2 * c, 2 * (c - LANES) + 1)
    return (r == src).astype(BF16)


def _expert_kernel(ie_ref, ib_ref, lo_ref, hi_ref, nv_ref, xs_ref, wgu_ref, bgu_ref, wd_ref, bd_ref,
                   y_ref, wgu_s, wd_s, y_s):
    i = pl.program_id(0)
    prev = jnp.maximum(i - 1, 0)
    valid = i < nv_ref[0]
    new_expert = (i == 0) | (ie_ref[i] != ie_ref[prev])
    new_block = (i == 0) | (ib_ref[i] != ib_ref[prev])
    n_groups = wgu_s.shape[1] // MXU_COLS
    n_strips = y_ref.shape[0]
    row = lax.broadcasted_iota(jnp.int32, (y_ref.shape[1], 1), 0)
    mine = (row >= lo_ref[i]) & (row < hi_ref[i])

    @pl.when(valid & new_expert)
    def _():
        perm = _split_gate_up_perm()
        for c in range(n_groups):
            cols = slice(c * MXU_COLS, (c + 1) * MXU_COLS)
            w = wgu_ref[0, :, cols].astype(BF16)
            wgu_s[:, cols] = jnp.dot(w, perm, preferred_element_type=F32).astype(BF16)
        wd_s[...] = wd_ref[0].astype(BF16)

    @pl.when(valid)
    def _():
        lo, hi = _unpack_pairs(jnp.concatenate([xs_ref[c] for c in range(n_strips)], axis=1))
        xb = jnp.concatenate([lo, hi], axis=1).astype(BF16)
        gu = jnp.dot(xb, wgu_s[...], preferred_element_type=F32) + bgu_ref[0]
        acts = []
        for c in range(n_groups):
            g = jnp.minimum(gu[:, c * MXU_COLS:c * MXU_COLS + LANES], SWIGLU_LIMIT)
            u = jnp.clip(gu[:, c * MXU_COLS + LANES:(c + 1) * MXU_COLS], -SWIGLU_LIMIT, SWIGLU_LIMIT)
            acts.append((g / (1.0 + jnp.exp(-SWIGLU_ALPHA * g)) * (u + 1.0)).astype(BF16))
        act = jnp.concatenate(acts, axis=1)
        y_s[...] = _pack_pairs(jnp.dot(act, wd_s[...], preferred_element_type=F32) + bd_ref[0])

    @pl.when(valid & new_block)
    def _():
        for c in range(n_strips):
            y_ref[c] = jnp.where(mine, y_s[:, c * STRIP:(c + 1) * STRIP], jnp.uint32(0))

    @pl.when(valid & jnp.logical_not(new_block))
    def _():
        for c in range(n_strips):
            y_ref[c] = jnp.where(mine, y_s[:, c * STRIP:(c + 1) * STRIP], y_ref[c])


def _expert_ffn(items, xs, w_gate_up, b_gate_up_grouped, w_down, b_down):
    n_strips, a, _ = xs.shape
    d = 2 * n_strips * STRIP
    dff2 = w_gate_up.shape[-1]
    dff = dff2 // 2
    n_items = items[0].shape[0]
    wmap = lambda i, ie, ib, lo, hi, nv: (ie[i], 0, 0)
    rmap = lambda i, ie, ib, lo, hi, nv: (0, ib[i], 0)
    grid_spec = pltpu.PrefetchScalarGridSpec(
        num_scalar_prefetch=5,
        grid=(n_items,),
        in_specs=[pl.BlockSpec((n_strips, MOE_ROWS, STRIP), rmap),
                  pl.BlockSpec((1, d, dff2), wmap), pl.BlockSpec((1, 1, dff2), wmap),
                  pl.BlockSpec((1, dff, d), wmap), pl.BlockSpec((1, 1, d), wmap)],
        out_specs=pl.BlockSpec((n_strips, MOE_ROWS, STRIP), rmap),
        scratch_shapes=[pltpu.VMEM((d, dff2), BF16), pltpu.VMEM((dff, d), BF16),
                        pltpu.VMEM((MOE_ROWS, d // 2), jnp.uint32)],
    )
    return pl.pallas_call(
        _expert_kernel,
        grid_spec=grid_spec,
        out_shape=jax.ShapeDtypeStruct((n_strips, a, STRIP), jnp.uint32),
        compiler_params=_params(("arbitrary",), EXPERT_VMEM_LIMIT),
        name="expert_ffn",
    )(*items, xs, w_gate_up, b_gate_up_grouped, w_down, b_down)


def _expert_work_items(counts, n_rows):
    e_ids = jnp.arange(N_EXPERTS, dtype=jnp.int32)
    ends = jnp.cumsum(counts).astype(jnp.int32)
    starts = ends - counts
    first_blk = starts // MOE_ROWS
    last_blk = (ends - 1) // MOE_ROWS
    per_expert = jnp.where(counts > 0, last_blk - first_blk + 1, 0)
    item_end = jnp.cumsum(per_expert).astype(jnp.int32)
    item_start = item_end - per_expert
    total = item_end[-1:]
    max_items = n_rows // MOE_ROWS + N_EXPERTS - 1
    it = jnp.clip(jnp.arange(max_items, dtype=jnp.int32), 0, jnp.maximum(total - 1, 0))
    expert = jnp.sum(item_end[None, :] <= it[:, None], axis=1).astype(jnp.int32)
    pick = lambda v: jnp.sum(jnp.where(expert[:, None] == e_ids, v, 0), axis=1)
    block = pick(first_blk) + it - pick(item_start)
    lo = jnp.maximum(pick(starts) - block * MOE_ROWS, 0)
    hi = jnp.minimum(pick(ends) - block * MOE_ROWS, MOE_ROWS)
    return expert, block, lo, hi, total, starts


def _gather_strips(strips, idx):
    c, r, w = strips.shape
    offsets = jnp.arange(c, dtype=jnp.int32)[:, None] * r
    out = _gather_rows(strips.reshape(c * r, w), (idx[None, :] + offsets).reshape(-1))
    return out.reshape(c, idx.shape[0], w)


def _gather_rows(table, idx):
    m = idx.shape[0]
    w = table.shape[1]
    window = GATHER_WINDOW
    mesh = plsc.VectorSubcoreMesh(core_axis_name="core", subcore_axis_name="subcore")

    @pl.kernel(out_type=jax.ShapeDtypeStruct((m, w), table.dtype), mesh=mesh)
    def gather_kernel(table_hbm, idx_hbm, out_hbm):
        def body(idx_vmem, out_vmem):
            pltpu.sync_copy(table_hbm.at[idx_vmem.at[0]], out_vmem)

        pltpu.emit_pipeline(
            body,
            grid=(m // window,),
            in_specs=[pl.BlockSpec((1, window), index_map=lambda i: (0, i))],
            out_specs=[pl.BlockSpec((window, w), index_map=lambda i: (i, 0))],
            core_axis_name=("core", "subcore"),
            dimension_semantics=(pltpu.PARALLEL,),
        )(idx_hbm, out_hbm)

    return gather_kernel(table, idx.reshape(1, m))


def _combine_kernel(yk_ref, tw_ref, x1_ref, mod_ref, gf_ref, o_ref):
    tw = tw_ref[0]
    lows, highs = [], []
    for c in range(yk_ref.shape[0]):
        lo, hi = _unpack_pairs(yk_ref[c, 0, 0])
        lo, hi = lo * tw[:, 0:1], hi * tw[:, 0:1]
        for k in range(1, TOP_K):
            lo_k, hi_k = _unpack_pairs(yk_ref[c, k, 0])
            lo, hi = lo + lo_k * tw[:, k:k + 1], hi + hi_k * tw[:, k:k + 1]
        lows.append(lo)
        highs.append(hi)
    y = jnp.concatenate(lows + highs, axis=1)
    x2 = x1_ref[0] + mod_ref[0][5:6] * y
    o_ref[0] = _rms(x2, gf_ref[...], 1e-6)


def _combine(yk, tw, x1, mod, g_final):
    b, s, d = x1.shape
    tm = min(TOKEN_TILE, s)
    gf = g_final.reshape(1, d)
    return pl.pallas_call(
        _combine_kernel,
        grid=(b, s // tm),
        in_specs=[pl.BlockSpec((d // (2 * STRIP), TOP_K, 1, tm, STRIP), lambda i, j: (0, 0, i, j, 0)),
                  pl.BlockSpec((1, tm, TOP_K), lambda i, j: (i, j, 0)),
                  pl.BlockSpec((1, tm, d), lambda i, j: (i, j, 0)),
                  pl.BlockSpec((1, N_MOD, d), lambda i, j: (i, 0, 0)),
                  pl.BlockSpec((1, d), lambda i, j: (0, 0))],
        out_specs=pl.BlockSpec((1, tm, d), lambda i, j: (i, j, 0)),
        out_shape=jax.ShapeDtypeStruct((b, s, d), F32),
        compiler_params=_params(("parallel", "parallel")),
        name="combine_final_norm",
    )(yk, tw, x1, mod, gf)


def kernel(x, c, positions, w_ada, b_ada, g_norm1, w_in, g_q_norm, w_uq, g_kv_norm, w_ukv,
           lambda_q1, lambda_k1, lambda_q2, lambda_k2, g_subln, w_out, g_norm2,
           w_router, b_router, w_gate_up, b_gate_up, w_down, b_down, g_final):
    b, s, d = x.shape
    n = b * s
    l = 0
    mod = _ada_mod(c, w_ada[l], b_ada[l]).reshape(b, N_MOD, d)

    w_in_p, w_dvt, w_uq_p, w_uk_p, w_uvt = _pad_in_weights(w_in[l], w_uq[l], w_ukv[l])
    qm, km, vmt, dq, dk, dvt = _in_projection(x, positions, mod, g_norm1[l], w_in_p, w_dvt,
                                              g_q_norm[l], w_uq_p, g_kv_norm[l], w_uk_p, w_uvt)
    o_mla = _mla_attention(qm, km, vmt)
    lam_vecs = jnp.stack([lambda_q1[l], lambda_k1[l], lambda_q2[l], lambda_k2[l]]).astype(F32)
    o_diff = _diff_attention(lam_vecs, g_subln[l], dq, dk, dvt)

    x1, h2, idx, tw, rank, cnt = _out_projection(o_mla, o_diff, x, mod, w_out[l], g_norm2[l],
                                                 w_router[l], b_router[l])

    to_kn = lambda a: a.transpose(1, 0, 2).reshape(TOP_K, n)
    idx, tw, rank = to_kn(idx), to_kn(tw), to_kn(rank)
    n_rows = n * TOP_K
    *items, starts = _expert_work_items(cnt[:, 0], n_rows)
    e_ids = jnp.arange(N_EXPERTS, dtype=jnp.int32)
    slot = rank + jnp.sum(jnp.where(idx[..., None] == e_ids, starts, 0), axis=-1)
    tok = jnp.arange(n, dtype=jnp.int32)
    row_tok = jnp.sort((idx * n + tok).reshape(-1)) % n

    xs = _gather_strips(h2.reshape(-1, n, STRIP), row_tok)
    bgu = (b_gate_up[l].reshape(N_EXPERTS, EXPERT_DFF // LANES, LANES, 2)
           .transpose(0, 1, 3, 2).reshape(N_EXPERTS, 1, 2 * EXPERT_DFF))
    y = _expert_ffn(items, xs, w_gate_up[l], bgu, w_down[l], b_down[l].reshape(N_EXPERTS, 1, d))

    yk = _gather_strips(y, slot.reshape(-1)).reshape(-1, TOP_K, b, s, STRIP)
    return _combine(yk, tw.T.reshape(b, s, TOP_K), x1, mod, g_final)
```

```python
import functools
import math

import jax
import jax.numpy as jnp
from jax import lax
from jax.experimental import pallas as pl
from jax.experimental.pallas import tpu as pltpu
from jax.experimental.pallas import tpu_sc as plsc

F32 = jnp.float32
BF16 = jnp.bfloat16
HIGHEST = lax.Precision.HIGHEST

D_MODEL = 1024
CHUNK = 64
ROPE_THETA = 500000.0
MLA_HEADS = 8
MLA_NOPE = 64
MLA_ROPE = 32
MLA_V = 64
MLA_QK = MLA_NOPE + MLA_ROPE
Q_RANK = 384
KV_RANK = 256
DIFF_HEADS = 4
DIFF_DIM = 64
DIFF_ROPE = DIFF_DIM // 4
DIFF_WIDTH = DIFF_HEADS * 2 * DIFF_DIM
N_EXPERTS = 32
TOP_K = 4
EXPERT_DFF = D_MODEL
SWIGLU_LIMIT = 7.0
SWIGLU_ALPHA = 1.702
N_MOD = 6
LAMBDA_INIT = 0.8 - 0.6 * math.exp(-0.3 * 0)
LOG2E = math.log2(math.e)

LANES = 128
MXU_COLS = 256
HEAD_PAD = LANES
TOKEN_TILE = 512
ATTN_TILE = 256
MOE_ROWS = 256
GATHER_WINDOW = 128
STRIP = 256
VMEM_LIMIT = 48 * 1024 * 1024
EXPERT_VMEM_LIMIT = 58 * 1024 * 1024
NEG_BIG = -1e30

MLA_X1 = 96
MLA_HALF = MLA_ROPE // 2
DIFF_HALF = DIFF_ROPE // 2

C_Q = 0
C_KV = Q_RANK
C_KPE = Q_RANK + KV_RANK
C_DQ = C_KPE + LANES
C_DK = C_DQ + DIFF_WIDTH
IN_COLS_PAD = C_DK + DIFF_WIDTH


def _rms(x, g, eps):
    return x * lax.rsqrt(jnp.mean(x * x, axis=-1, keepdims=True) + eps) * g


def _pack_pairs(x):
    w = x.shape[1] // 2
    as_bits = lambda v: lax.bitcast_convert_type(v.astype(BF16).astype(F32), jnp.uint32)
    return (as_bits(x[:, w:]) & jnp.uint32(0xFFFF0000)) | (as_bits(x[:, :w]) >> 16)


def _unpack_pairs(p):
    lo = lax.bitcast_convert_type(p << 16, F32)
    hi = lax.bitcast_convert_type(p & jnp.uint32(0xFFFF0000), F32)
    return lo, hi


def _params(sem, limit=VMEM_LIMIT):
    return pltpu.CompilerParams(dimension_semantics=sem, vmem_limit_bytes=limit)


def _nt(a, b):
    return lax.dot_general(a, b, (((1,), (1,)), ((), ())), preferred_element_type=F32)


def _ada_kernel(c_ref, w_ref, b_ref, o_ref):
    c = c_ref[...]
    sc = c / (1.0 + jnp.exp(-c))
    o_ref[...] = jnp.dot(sc, w_ref[...], preferred_element_type=F32, precision=HIGHEST) + b_ref[...]


def _ada_mod(c, w_ada, b_ada):
    b, d = c.shape
    n = w_ada.shape[1]
    return pl.pallas_call(
        _ada_kernel,
        grid=(n // d,),
        in_specs=[pl.BlockSpec((b, d), lambda j: (0, 0)),
                  pl.BlockSpec((d, d), lambda j: (0, j)),
                  pl.BlockSpec((1, d), lambda j: (0, j))],
        out_specs=pl.BlockSpec((b, d), lambda j: (0, j)),
        out_shape=jax.ShapeDtypeStruct((b, n), F32),
        compiler_params=_params(("arbitrary",)),
        name="ada_mod",
    )(c, w_ada, b_ada.reshape(1, n))


def _rope(t, cos, sin_a, sin_b, shift):
    return (t * cos + pltpu.roll(t, LANES - shift, 1) * sin_a + pltpu.roll(t, shift, 1) * sin_b)


def _inproj_kernel(x_ref, pos_ref, mod_ref, g1_ref, win_ref, wdvt_ref, gq_ref, wuq_ref, gkv_ref,
                   wuk_ref, wuvt_ref, freq_ref, mask_ref,
                   qm_ref, km_ref, vmt_ref, dq_ref, dk_ref, dvt_ref):
    x = x_ref[0]
    mod = mod_ref[0]
    h = (_rms(x, g1_ref[...], 1e-6) * (1.0 + mod[1:2]) + mod[0:1]).astype(BF16)
    proj = jnp.dot(h, win_ref[...], preferred_element_type=F32)
    dvt_ref[0] = _nt(wdvt_ref[...], h).astype(BF16)

    ang = pos_ref[0].astype(F32) * freq_ref[...]
    cos, sin = jnp.cos(ang), jnp.sin(ang)
    mk = mask_ref[...]
    cos_m = jnp.where(mk[0:1] > 0, cos, 1.0)
    sa_m, sb_m = sin * mk[1:2], sin * mk[2:3]
    cos_d = jnp.where(mk[3:4] > 0, cos, 1.0)
    sa_d, sb_d = sin * mk[4:5], sin * mk[5:6]
    mla_scale = MLA_QK ** -0.5 * LOG2E
    diff_scale = DIFF_DIM ** -0.5 * LOG2E

    cq = _rms(proj[:, C_Q:C_Q + Q_RANK], gq_ref[...], 1e-6)
    q = jnp.dot(cq.astype(BF16), wuq_ref[...], preferred_element_type=F32)
    ckv = _rms(proj[:, C_KV:C_KV + KV_RANK], gkv_ref[...], 1e-6).astype(BF16)
    kn = jnp.dot(ckv, wuk_ref[...], preferred_element_type=F32)
    vmt_ref[0] = _nt(wuvt_ref[...], ckv).astype(BF16)
    kpe = _rope(proj[:, C_KPE:C_KPE + LANES], cos_m, sa_m, sb_m, MLA_HALF)
    cq_m, saq_m, sbq_m = cos_m * mla_scale, sa_m * mla_scale, sb_m * mla_scale
    for hd in range(MLA_HEADS):
        sl = slice(hd * HEAD_PAD, (hd + 1) * HEAD_PAD)
        qm_ref[0, :, sl] = _rope(q[:, sl], cq_m, saq_m, sbq_m, MLA_HALF).astype(BF16)
        km_ref[0, :, sl] = (kn[:, sl] + kpe).astype(BF16)

    cq_d, saq_d, sbq_d = cos_d * diff_scale, sa_d * diff_scale, sb_d * diff_scale
    for hd in range(DIFF_HEADS):
        sl = slice(hd * LANES, (hd + 1) * LANES)
        tq = proj[:, C_DQ + hd * LANES:C_DQ + (hd + 1) * LANES]
        tk = proj[:, C_DK + hd * LANES:C_DK + (hd + 1) * LANES]
        dq_ref[0, :, sl] = _rope(tq, cq_d, saq_d, sbq_d, DIFF_HALF).astype(BF16)
        dk_ref[0, :, sl] = _rope(tk, cos_d, sa_d, sb_d, DIFF_HALF).astype(BF16)


def _rope_tables():
    lane = jnp.arange(LANES)
    f_m = ROPE_THETA ** (-jnp.arange(MLA_HALF, dtype=F32) / MLA_HALF)
    f_d = ROPE_THETA ** (-jnp.arange(DIFF_HALF, dtype=F32) / DIFF_HALF)
    m_x1 = (lane >= MLA_X1) & (lane < MLA_X1 + MLA_HALF)
    m_x2 = (lane >= MLA_X1 + MLA_HALF) & (lane < MLA_X1 + 2 * MLA_HALF)
    d_x1 = (lane % DIFF_DIM) < DIFF_HALF
    d_x2 = ((lane % DIFF_DIM) >= DIFF_HALF) & ((lane % DIFF_DIM) < 2 * DIFF_HALF)
    freq = jnp.where(m_x1 | m_x2, f_m[(lane - MLA_X1) % MLA_HALF], 0.0)
    freq = jnp.where(d_x1 | d_x2, f_d[lane % DIFF_HALF], freq)
    z = jnp.zeros((LANES,), F32)
    masks = jnp.stack([(m_x1 | m_x2).astype(F32), -m_x1.astype(F32), m_x2.astype(F32),
                       (d_x1 | d_x2).astype(F32), -d_x1.astype(F32), d_x2.astype(F32), z, z])
    return freq.reshape(1, LANES).astype(F32), masks


def _pad_in_weights(w_in, w_uq, w_ukv):
    d = w_in.shape[0]
    c_kpe = Q_RANK + KV_RANK
    c_dv = c_kpe + MLA_ROPE + 2 * DIFF_WIDTH
    kpe = jnp.concatenate([jnp.zeros((d, MLA_X1), F32), w_in[:, c_kpe:c_kpe + MLA_ROPE]], axis=1)
    w_in_p = jnp.concatenate([w_in[:, :c_kpe], kpe, w_in[:, c_kpe + MLA_ROPE:c_dv]], axis=1)
    w_dvt = w_in[:, c_dv:].T
    uq = w_uq.reshape(Q_RANK, MLA_HEADS, MLA_QK)
    uq_p = jnp.concatenate([uq[:, :, :MLA_NOPE],
                            jnp.zeros((Q_RANK, MLA_HEADS, MLA_X1 - MLA_NOPE), F32),
                            uq[:, :, MLA_NOPE:]], axis=2).reshape(Q_RANK, MLA_HEADS * HEAD_PAD)
    ukv = w_ukv.reshape(KV_RANK, MLA_HEADS, MLA_NOPE + MLA_V)
    uk_p = jnp.concatenate([ukv[:, :, :MLA_NOPE],
                            jnp.zeros((KV_RANK, MLA_HEADS, HEAD_PAD - MLA_NOPE), F32)],
                           axis=2).reshape(KV_RANK, MLA_HEADS * HEAD_PAD)
    uvt = ukv[:, :, MLA_NOPE:].reshape(KV_RANK, MLA_HEADS * MLA_V).T
    return (w_in_p.astype(BF16), w_dvt.astype(BF16), uq_p.astype(BF16), uk_p.astype(BF16),
            uvt.astype(BF16))


def _in_projection(x, positions, mod, g_norm1, w_in_p, w_dvt, g_q, w_uq_p, g_kv, w_uk_p, w_uvt):
    b, s, d = x.shape
    tm = min(TOKEN_TILE, s)
    freq, masks = _rope_tables()
    full = lambda a: pl.BlockSpec(a.shape, lambda i, j: (0,) * a.ndim)
    tile = lambda w: pl.BlockSpec((1, tm, w), lambda i, j: (i, j, 0))
    tile_t = lambda w: pl.BlockSpec((1, w, tm), lambda i, j: (i, 0, j))
    wide, vm_w = MLA_HEADS * HEAD_PAD, MLA_HEADS * MLA_V
    g1, gq, gkv = g_norm1.reshape(1, d), g_q.reshape(1, Q_RANK), g_kv.reshape(1, KV_RANK)
    tok = lambda w: jax.ShapeDtypeStruct((b, s, w), BF16)
    feat = lambda w: jax.ShapeDtypeStruct((b, w, s), BF16)
    return pl.pallas_call(
        _inproj_kernel,
        grid=(b, s // tm),
        in_specs=[tile(d), tile(1), pl.BlockSpec((1, N_MOD, d), lambda i, j: (i, 0, 0)),
                  full(g1), full(w_in_p), full(w_dvt), full(gq), full(w_uq_p), full(gkv),
                  full(w_uk_p), full(w_uvt), full(freq), full(masks)],
        out_specs=[tile(wide), tile(wide), tile_t(vm_w), tile(DIFF_WIDTH), tile(DIFF_WIDTH),
                   tile_t(DIFF_WIDTH)],
        out_shape=[tok(wide), tok(wide), feat(vm_w), tok(DIFF_WIDTH), tok(DIFF_WIDTH),
                   feat(DIFF_WIDTH)],
        compiler_params=_params(("parallel", "parallel")),
        name="in_projection",
    )(x, positions.reshape(b, s, 1), mod, g1, w_in_p, w_dvt, gq, w_uq_p, gkv, w_uk_p, w_uvt,
      freq, masks)


def _diag_mask(t):
    key = lax.broadcasted_iota(jnp.int32, (t, t), 0) // CHUNK
    qry = lax.broadcasted_iota(jnp.int32, (t, t), 1) // CHUNK
    return key <= qry


def _scores_t(q, k_ref, lanes, q0, t):
    diag = jnp.where(_diag_mask(t), _nt(k_ref[0, q0:q0 + t, lanes], q), NEG_BIG)
    bulk = _nt(k_ref[0, 0:q0, lanes], q) if q0 else None
    return bulk, diag


def _softmax_t(bulk, diag):
    m = jnp.max(diag, axis=0, keepdims=True)
    if bulk is not None:
        m = jnp.maximum(m, jnp.max(bulk, axis=0, keepdims=True))
    p_diag = jnp.exp2(diag - m)
    l = jnp.sum(p_diag, axis=0, keepdims=True)
    p_bulk = None
    if bulk is not None:
        p_bulk = jnp.exp2(bulk - m)
        l = l + jnp.sum(p_bulk, axis=0, keepdims=True)
    return p_bulk, p_diag, l


def _pv_t(vt_ref, rows, p_bulk, p_diag, q0, t):
    o = jnp.dot(vt_ref[0, rows, q0:q0 + t], p_diag.astype(BF16), preferred_element_type=F32)
    if p_bulk is not None:
        o = o + jnp.dot(vt_ref[0, rows, 0:q0], p_bulk.astype(BF16), preferred_element_type=F32)
    return o


def _mla_attn_kernel(q_ref, k_ref, vt_ref, o_ref, *, seq, t):
    units = [(q0, i) for q0 in range(0, seq, t) for i in range(2)]

    def scores(q0, i):
        lanes = slice(i * HEAD_PAD, (i + 1) * HEAD_PAD)
        return _scores_t(q_ref[0, q0:q0 + t, lanes], k_ref, lanes, q0, t)

    ahead = scores(*units[0])
    outs = []
    for u, (q0, i) in enumerate(units):
        bulk, diag = ahead
        ahead = scores(*units[u + 1]) if u + 1 < len(units) else None
        p_bulk, p_diag, l = _softmax_t(bulk, diag)
        rows = slice(i * MLA_V, (i + 1) * MLA_V)
        outs.append(_pv_t(vt_ref, rows, p_bulk, p_diag, q0, t) / l)
        if i == 1:
            o_ref[0, q0:q0 + t, :] = jnp.concatenate(outs, axis=0).T.astype(BF16)
            outs = []


def _diff_attn_kernel(lam_ref, gs_ref, q_ref, k_ref, vt_ref, o_ref, *, seq, t):
    lv = lam_ref[...]
    lam = (jnp.exp(jnp.sum(lv[0:1] * lv[1:2], axis=-1, keepdims=True))
           - jnp.exp(jnp.sum(lv[2:3] * lv[3:4], axis=-1, keepdims=True)) + LAMBDA_INIT)
    lane = lax.broadcasted_iota(jnp.int32, (t, 2 * DIFF_DIM), 1)
    every = slice(0, 2 * DIFF_DIM)
    def scores(q0):
        q = q_ref[0, q0:q0 + t, :]
        zero = jnp.zeros_like(q)
        return (_scores_t(jnp.where(lane < DIFF_DIM, q, zero), k_ref, every, q0, t),
                _scores_t(jnp.where(lane >= DIFF_DIM, q, zero), k_ref, every, q0, t))

    ahead = scores(0)
    for q0 in range(0, seq, t):
        (b1, d1), (b2, d2) = ahead
        ahead = scores(q0 + t) if q0 + t < seq else None
        pb1, pd1, l1 = _softmax_t(b1, d1)
        pb2, pd2, l2 = _softmax_t(b2, d2)
        c1, c2 = 1.0 / l1, lam / l2
        w_diag = pd1 * c1 - pd2 * c2
        w_bulk = (pb1 * c1 - pb2 * c2) if q0 else None
        o = _pv_t(vt_ref, every, w_bulk, w_diag, q0, t)
        o = (o * lax.rsqrt(jnp.mean(o * o, axis=0, keepdims=True) + 1e-5) * gs_ref[...]
             * (1.0 - LAMBDA_INIT))
        o_ref[0, q0:q0 + t, :] = o.T.astype(BF16)


def _mla_attention(qm, km, vmt):
    b, s, _ = qm.shape
    t = min(ATTN_TILE, s)
    pair = 2 * HEAD_PAD
    return pl.pallas_call(
        functools.partial(_mla_attn_kernel, seq=s, t=t),
        grid=(b, MLA_HEADS // 2),
        in_specs=[pl.BlockSpec((1, s, pair), lambda i, j: (i, 0, j)),
                  pl.BlockSpec((1, s, pair), lambda i, j: (i, 0, j)),
                  pl.BlockSpec((1, LANES, s), lambda i, j: (i, j, 0))],
        out_specs=pl.BlockSpec((1, s, LANES), lambda i, j: (i, 0, j)),
        out_shape=jax.ShapeDtypeStruct((b, s, MLA_HEADS * MLA_V), BF16),
        compiler_params=_params(("parallel", "parallel")),
        name="mla_attention",
    )(qm, km, vmt)


def _diff_attention(lam_vecs, g_subln, dq, dk, dvt):
    b, s, _ = dq.shape
    t = min(ATTN_TILE, s)
    blk = pl.BlockSpec((1, s, LANES), lambda i, j: (i, 0, j))
    gs = g_subln.reshape(2 * DIFF_DIM, 1)
    return pl.pallas_call(
        functools.partial(_diff_attn_kernel, seq=s, t=t),
        grid=(b, DIFF_HEADS),
        in_specs=[pl.BlockSpec(lam_vecs.shape, lambda i, j: (0, 0)),
                  pl.BlockSpec(gs.shape, lambda i, j: (0, 0)), blk, blk,
                  pl.BlockSpec((1, LANES, s), lambda i, j: (i, j, 0))],
        out_specs=blk,
        out_shape=jax.ShapeDtypeStruct((b, s, DIFF_WIDTH), BF16),
        compiler_params=_params(("parallel", "parallel")),
        name="diff_attention",
    )(lam_vecs, gs, dq, dk, dvt)


def _outproj_kernel(om_ref, od_ref, x_ref, mod_ref, wout_ref, g2_ref, wr_ref, br_ref,
                    x1_ref, h2_ref, idx_ref, tw_ref, rank_ref, cnt_ref, carry_ref):
    first = (pl.program_id(0) == 0) & (pl.program_id(1) == 0)

    @pl.when(first)
    def _():
        carry_ref[...] = jnp.zeros(carry_ref.shape, F32)

    half = om_ref.shape[-1]
    mix = (jnp.dot(om_ref[0], wout_ref[:half, :], preferred_element_type=F32)
           + jnp.dot(od_ref[0], wout_ref[half:, :], preferred_element_type=F32))
    mod = mod_ref[0]
    x1 = x_ref[0] + mod[2:3] * mix
    h2 = _rms(x1, g2_ref[...], 1e-6) * (1.0 + mod[4:5]) + mod[3:4]
    x1_ref[0] = x1
    packed = _pack_pairs(h2)
    for c in range(h2_ref.shape[0]):
        h2_ref[c, 0] = packed[:, c * STRIP:(c + 1) * STRIP]

    logits = lax.dot_general(wr_ref[...], h2, (((1,), (1,)), ((), ())),
                             preferred_element_type=F32, precision=HIGHEST) + br_ref[...]
    n_e, t = logits.shape
    e_iota = lax.broadcasted_iota(jnp.int32, (n_e, t), 0)
    vals, idxs = [], []
    rest = logits
    for _ in range(TOP_K):
        m = jnp.max(rest, axis=0, keepdims=True)
        ik = jnp.min(jnp.where(rest == m, e_iota, n_e), axis=0, keepdims=True)
        vals.append(m)
        idxs.append(ik)
        rest = jnp.where(e_iota == ik, -jnp.inf, rest)
    ex = [jnp.exp(v - vals[0]) for v in vals]
    den = ex[0] + ex[1] + ex[2] + ex[3]
    sel = (e_iota == idxs[0]) | (e_iota == idxs[1]) | (e_iota == idxs[2]) | (e_iota == idxs[3])

    before = (lax.broadcasted_iota(jnp.int32, (t, t), 0)
              < lax.broadcasted_iota(jnp.int32, (t, t), 1)).astype(BF16)
    prefix = jnp.dot(sel.astype(BF16), before, preferred_element_type=F32)
    pos = carry_ref[...] + prefix
    for k in range(TOP_K):
        idx_ref[0, k:k + 1, :] = idxs[k]
        tw_ref[0, k:k + 1, :] = ex[k] / den
        rk = jnp.sum(jnp.where(e_iota == idxs[k], pos, 0.0), axis=0, keepdims=True)
        rank_ref[0, k:k + 1, :] = rk.astype(jnp.int32)
    total = carry_ref[...] + jnp.sum(sel.astype(F32), axis=1, keepdims=True)
    carry_ref[...] = total
    cnt_ref[...] = total.astype(jnp.int32)


def _out_projection(o_mla, o_diff, x, mod, w_out, g_norm2, w_router, b_router):
    b, s, d = x.shape
    tm = min(TOKEN_TILE, s)
    nt = s // tm
    half = o_mla.shape[-1]
    full = lambda a: pl.BlockSpec(a.shape, lambda i, j: (0,) * a.ndim)
    tile = lambda w: pl.BlockSpec((1, tm, w), lambda i, j: (i, j, 0))
    route = pl.BlockSpec((1, TOP_K, tm), lambda i, j: (i * nt + j, 0, 0))
    g2 = g_norm2.reshape(1, d)
    wr = w_router.T
    br = b_router.reshape(N_EXPERTS, 1)
    w_out_b = w_out.astype(BF16)
    n_strips = d // (2 * STRIP)
    return pl.pallas_call(
        _outproj_kernel,
        grid=(b, nt),
        in_specs=[tile(half), tile(half), tile(d), pl.BlockSpec((1, N_MOD, d), lambda i, j: (i, 0, 0)),
                  full(w_out_b), full(g2), full(wr), full(br)],
        out_specs=[tile(d), pl.BlockSpec((n_strips, 1, tm, STRIP), lambda i, j: (0, i, j, 0)),
                   route, route, route,
                   pl.BlockSpec((N_EXPERTS, 1), lambda i, j: (0, 0))],
        out_shape=[jax.ShapeDtypeStruct((b, s, d), F32),
                   jax.ShapeDtypeStruct((n_strips, b, s, STRIP), jnp.uint32),
                   jax.ShapeDtypeStruct((b * nt, TOP_K, tm), jnp.int32),
                   jax.ShapeDtypeStruct((b * nt, TOP_K, tm), F32),
                   jax.ShapeDtypeStruct((b * nt, TOP_K, tm), jnp.int32),
                   jax.ShapeDtypeStruct((N_EXPERTS, 1), jnp.int32)],
        scratch_shapes=[pltpu.VMEM((N_EXPERTS, 1), F32)],
        compiler_params=_params(("arbitrary", "arbitrary")),
        name="out_projection_router",
    )(o_mla, o_diff, x, mod, w_out_b, g2, wr, br)


def _split_gate_up_perm():
    r = lax.broadcasted_iota(jnp.int32, (MXU_COLS, MXU_COLS), 0)
    c = lax.broadcasted_iota(jnp.int32, (MXU_COLS, MXU_COLS), 1)
    src = jnp.where(c < LANES, 2 * c, 2 * (c - LANES) + 1)
    return (r == src).astype(BF16)


def _expert_kernel(ie_ref, ib_ref, lo_ref, hi_ref, nv_ref, xs_ref, wgu_ref, bgu_ref, wd_ref, bd_ref,
                   y_ref, wgu_s, wd_s, y_s):
    i = pl.program_id(0)
    prev = jnp.maximum(i - 1, 0)
    valid = i < nv_ref[0]
    new_expert = (i == 0) | (ie_ref[i] != ie_ref[prev])
    new_block = (i == 0) | (ib_ref[i] != ib_ref[prev])
    n_groups = wgu_s.shape[1] // MXU_COLS
    n_strips = y_ref.shape[0]
    row = lax.broadcasted_iota(jnp.int32, (y_ref.shape[1], 1), 0)
    mine = (row >= lo_ref[i]) & (row < hi_ref[i])

    @pl.when(valid & new_expert)
    def _():
        perm = _split_gate_up_perm()
        for c in range(n_groups):
            cols = slice(c * MXU_COLS, (c + 1) * MXU_COLS)
            w = wgu_ref[0, :, cols].astype(BF16)
            wgu_s[:, cols] = jnp.dot(w, perm, preferred_element_type=F32).astype(BF16)
        wd_s[...] = wd_ref[0].astype(BF16)

    @pl.when(valid)
    def _():
        lo, hi = _unpack_pairs(jnp.concatenate([xs_ref[c] for c in range(n_strips)], axis=1))
        xb = jnp.concatenate([lo, hi], axis=1).astype(BF16)
        gu = jnp.dot(xb, wgu_s[...], preferred_element_type=F32) + bgu_ref[0]
        acts = []
        for c in range(n_groups):
            g = jnp.minimum(gu[:, c * MXU_COLS:c * MXU_COLS + LANES], SWIGLU_LIMIT)
            u = jnp.clip(gu[:, c * MXU_COLS + LANES:(c + 1) * MXU_COLS], -SWIGLU_LIMIT, SWIGLU_LIMIT)
            acts.append((g / (1.0 + jnp.exp(-SWIGLU_ALPHA * g)) * (u + 1.0)).astype(BF16))
        act = jnp.concatenate(acts, axis=1)
        y_s[...] = _pack_pairs(jnp.dot(act, wd_s[...], preferred_element_type=F32) + bd_ref[0])

    @pl.when(valid & new_block)
    def _():
        for c in range(n_strips):
            y_ref[c] = jnp.where(mine, y_s[:, c * STRIP:(c + 1) * STRIP], jnp.uint32(0))

    @pl.when(valid & jnp.logical_not(new_block))
    def _():
        for c in range(n_strips):
            y_ref[c] = jnp.where(mine, y_s[:, c * STRIP:(c + 1) * STRIP], y_ref[c])


def _expert_ffn(items, xs, w_gate_up, b_gate_up_grouped, w_down, b_down):
    n_strips, a, _ = xs.shape
    d = 2 * n_strips * STRIP
    dff2 = w_gate_up.shape[-1]
    dff = dff2 // 2
    n_items = items[0].shape[0]
    wmap = lambda i, ie, ib, lo, hi, nv: (ie[i], 0, 0)
    rmap = lambda i, ie, ib, lo, hi, nv: (0, ib[i], 0)
    grid_spec = pltpu.PrefetchScalarGridSpec(
        num_scalar_prefetch=5,
        grid=(n_items,),
        in_specs=[pl.BlockSpec((n_strips, MOE_ROWS, STRIP), rmap),
                  pl.BlockSpec((1, d, dff2), wmap), pl.BlockSpec((1, 1, dff2), wmap),
                  pl.BlockSpec((1, dff, d), wmap), pl.BlockSpec((1, 1, d), wmap)],
        out_specs=pl.BlockSpec((n_strips, MOE_ROWS, STRIP), rmap),
        scratch_shapes=[pltpu.VMEM((d, dff2), BF16), pltpu.VMEM((dff, d), BF16),
                        pltpu.VMEM((MOE_ROWS, d // 2), jnp.uint32)],
    )
    return pl.pallas_call(
        _expert_kernel,
        grid_spec=grid_spec,
        out_shape=jax.ShapeDtypeStruct((n_strips, a, STRIP), jnp.uint32),
        compiler_params=_params(("arbitrary",), EXPERT_VMEM_LIMIT),
        name="expert_ffn",
    )(*items, xs, w_gate_up, b_gate_up_grouped, w_down, b_down)


def _expert_work_items(counts, n_rows):
    e_ids = jnp.arange(N_EXPERTS, dtype=jnp.int32)
    ends = jnp.cumsum(counts).astype(jnp.int32)
    starts = ends - counts
    first_blk = starts // MOE_ROWS
    last_blk = (ends - 1) // MOE_ROWS
    per_expert = jnp.where(counts > 0, last_blk - first_blk + 1, 0)
    item_end = jnp.cumsum(per_expert).astype(jnp.int32)
    item_start = item_end - per_expert
    total = item_end[-1:]
    max_items = n_rows // MOE_ROWS + N_EXPERTS - 1
    it = jnp.clip(jnp.arange(max_items, dtype=jnp.int32), 0, jnp.maximum(total - 1, 0))
    expert = jnp.sum(item_end[None, :] <= it[:, None], axis=1).astype(jnp.int32)
    pick = lambda v: jnp.sum(jnp.where(expert[:, None] == e_ids, v, 0), axis=1)
    block = pick(first_blk) + it - pick(item_start)
    lo = jnp.maximum(pick(starts) - block * MOE_ROWS, 0)
    hi = jnp.minimum(pick(ends) - block * MOE_ROWS, MOE_ROWS)
    return expert, block, lo, hi, total, starts


def _gather_strips(strips, idx):
    c, r, w = strips.shape
    offsets = jnp.arange(c, dtype=jnp.int32)[:, None] * r
    out = _gather_rows(strips.reshape(c * r, w), (idx[None, :] + offsets).reshape(-1))
    return out.reshape(c, idx.shape[0], w)


def _gather_rows(table, idx):
    m = idx.shape[0]
    w = table.shape[1]
    window = GATHER_WINDOW
    mesh = plsc.VectorSubcoreMesh(core_axis_name="core", subcore_axis_name="subcore")

    @pl.kernel(out_type=jax.ShapeDtypeStruct((m, w), table.dtype), mesh=mesh)
    def gather_kernel(table_hbm, idx_hbm, out_hbm):
        def body(idx_vmem, out_vmem):
            pltpu.sync_copy(table_hbm.at[idx_vmem.at[0]], out_vmem)

        pltpu.emit_pipeline(
            body,
            grid=(m // window,),
            in_specs=[pl.BlockSpec((1, window), index_map=lambda i: (0, i))],
            out_specs=[pl.BlockSpec((window, w), index_map=lambda i: (i, 0))],
            core_axis_name=("core", "subcore"),
            dimension_semantics=(pltpu.PARALLEL,),
        )(idx_hbm, out_hbm)

    return gather_kernel(table, idx.reshape(1, m))


def _combine_kernel(yk_ref, tw_ref, x1_ref, mod_ref, gf_ref, o_ref):
    tw = tw_ref[0]
    lows, highs = [], []
    for c in range(yk_ref.shape[0]):
        lo, hi = _unpack_pairs(yk_ref[c, 0, 0])
        lo, hi = lo * tw[:, 0:1], hi * tw[:, 0:1]
        for k in range(1, TOP_K):
            lo_k, hi_k = _unpack_pairs(yk_ref[c, k, 0])
            lo, hi = lo + lo_k * tw[:, k:k + 1], hi + hi_k * tw[:, k:k + 1]
        lows.append(lo)
        highs.append(hi)
    y = jnp.concatenate(lows + highs, axis=1)
    x2 = x1_ref[0] + mod_ref[0][5:6] * y
    o_ref[0] = _rms(x2, gf_ref[...], 1e-6)


def _combine(yk, tw, x1, mod, g_final):
    b, s, d = x1.shape
    tm = min(TOKEN_TILE, s)
    gf = g_final.reshape(1, d)
    return pl.pallas_call(
        _combine_kernel,
        grid=(b, s // tm),
        in_specs=[pl.BlockSpec((d // (2 * STRIP), TOP_K, 1, tm, STRIP), lambda i, j: (0, 0, i, j, 0)),
                  pl.BlockSpec((1, tm, TOP_K), lambda i, j: (i, j, 0)),
                  pl.BlockSpec((1, tm, d), lambda i, j: (i, j, 0)),
                  pl.BlockSpec((1, N_MOD, d), lambda i, j: (i, 0, 0)),
                  pl.BlockSpec((1, d), lambda i, j: (0, 0))],
        out_specs=pl.BlockSpec((1, tm, d), lambda i, j: (i, j, 0)),
        out_shape=jax.ShapeDtypeStruct((b, s, d), F32),
        compiler_params=_params(("parallel", "parallel")),
        name="combine_final_norm",
    )(yk, tw, x1, mod, gf)


def kernel(x, c, positions, w_ada, b_ada, g_norm1, w_in, g_q_norm, w_uq, g_kv_norm, w_ukv,
           lambda_q1, lambda_k1, lambda_q2, lambda_k2, g_subln, w_out, g_norm2,
           w_router, b_router, w_gate_up, b_gate_up, w_down, b_down, g_final):
    b, s, d = x.shape
    n = b * s
    l = 0
    mod = _ada_mod(c, w_ada[l], b_ada[l]).reshape(b, N_MOD, d)

    w_in_p, w_dvt, w_uq_p, w_uk_p, w_uvt = _pad_in_weights(w_in[l], w_uq[l], w_ukv[l])
    qm, km, vmt, dq, dk, dvt = _in_projection(x, positions, mod, g_norm1[l], w_in_p, w_dvt,
                                              g_q_norm[l], w_uq_p, g_kv_norm[l], w_uk_p, w_uvt)
    o_mla = _mla_attention(qm, km, vmt)
    lam_vecs = jnp.stack([lambda_q1[l], lambda_k1[l], lambda_q2[l], lambda_k2[l]]).astype(F32)
    o_diff = _diff_attention(lam_vecs, g_subln[l], dq, dk, dvt)

    x1, h2, idx, tw, rank, cnt = _out_projection(o_mla, o_diff, x, mod, w_out[l], g_norm2[l],
                                                 w_router[l], b_router[l])

    to_kn = lambda a: a.transpose(1, 0, 2).reshape(TOP_K, n)
    idx, tw, rank = to_kn(idx), to_kn(tw), to_kn(rank)
    n_rows = n * TOP_K
    *items, starts = _expert_work_items(cnt[:, 0], n_rows)
    e_ids = jnp.arange(N_EXPERTS, dtype=jnp.int32)
    slot = rank + jnp.sum(jnp.where(idx[..., None] == e_ids, starts, 0), axis=-1)
    tok = jnp.arange(n, dtype=jnp.int32)
    row_tok = jnp.sort((idx * n + tok).reshape(-1)) % n

    xs = _gather_strips(h2.reshape(-1, n, STRIP), row_tok)
    bgu = (b_gate_up[l].reshape(N_EXPERTS, EXPERT_DFF // LANES, LANES, 2)
           .transpose(0, 1, 3, 2).reshape(N_EXPERTS, 1, 2 * EXPERT_DFF))
    y = _expert_ffn(items, xs, w_gate_up[l], bgu, w_down[l], b_down[l].reshape(N_EXPERTS, 1, d))

    yk = _gather_strips(y, slot.reshape(-1)).reshape(-1, TOP_K, b, s, STRIP)
    return _combine(yk, tw.T.reshape(b, s, TOP_K), x1, mod, g_final)
```

```python
import functools
import math

import jax
import jax.numpy as jnp
from jax import lax
from jax.experimental import pallas as pl
from jax.experimental.pallas import tpu as pltpu
from jax.experimental.pallas import tpu_sc as plsc

F32 = jnp.float32
BF16 = jnp.bfloat16
HIGHEST = lax.Precision.HIGHEST

D_MODEL = 1024
CHUNK = 64
ROPE_THETA = 500000.0
MLA_HEADS = 8
MLA_NOPE = 64
MLA_ROPE = 32
MLA_V = 64
MLA_QK = MLA_NOPE + MLA_ROPE
Q_RANK = 384
KV_RANK = 256
DIFF_HEADS = 4
DIFF_DIM = 64
DIFF_ROPE = DIFF_DIM // 4
DIFF_WIDTH = DIFF_HEADS * 2 * DIFF_DIM
N_EXPERTS = 32
TOP_K = 4
EXPERT_DFF = D_MODEL
SWIGLU_LIMIT = 7.0
SWIGLU_ALPHA = 1.702
N_MOD = 6
LAMBDA_INIT = 0.8 - 0.6 * math.exp(-0.3 * 0)
LOG2E = math.log2(math.e)

LANES = 128
MXU_COLS = 256
HEAD_PAD = LANES
TOKEN_TILE = 512
MLA_TILE = 512
DIFF_TILE = 256
BATCH_GROUPS = 2
MOE_ROWS = 256
GATHER_WINDOW = 128
STRIP = 256
VMEM_LIMIT = 48 * 1024 * 1024
EXPERT_VMEM_LIMIT = 58 * 1024 * 1024
NEG_BIG = -1e30

MLA_X1 = 96
MLA_HALF = MLA_ROPE // 2
DIFF_HALF = DIFF_ROPE // 2

C_Q = 0
C_KV = Q_RANK
C_KPE = Q_RANK + KV_RANK
C_DQ = C_KPE + LANES
C_DK = C_DQ + DIFF_WIDTH
IN_COLS_PAD = C_DK + DIFF_WIDTH


def _rms(x, g, eps):
    return x * lax.rsqrt(jnp.mean(x * x, axis=-1, keepdims=True) + eps) * g


def _pack_pairs(x):
    w = x.shape[1] // 2
    as_bits = lambda v: lax.bitcast_convert_type(v.astype(BF16).astype(F32), jnp.uint32)
    return (as_bits(x[:, w:]) & jnp.uint32(0xFFFF0000)) | (as_bits(x[:, :w]) >> 16)


def _unpack_pairs(p):
    lo = lax.bitcast_convert_type(p << 16, F32)
    hi = lax.bitcast_convert_type(p & jnp.uint32(0xFFFF0000), F32)
    return lo, hi


def _params(sem, limit=VMEM_LIMIT):
    return pltpu.CompilerParams(dimension_semantics=sem, vmem_limit_bytes=limit)


def _nt(a, b):
    return lax.dot_general(a, b, (((1,), (1,)), ((), ())), preferred_element_type=F32)


def _ada_kernel(c_ref, w_ref, b_ref, o_ref):
    c = c_ref[...]
    sc = c / (1.0 + jnp.exp(-c))
    o_ref[...] = jnp.dot(sc, w_ref[...], preferred_element_type=F32, precision=HIGHEST) + b_ref[...]


def _ada_mod(c, w_ada, b_ada):
    b, d = c.shape
    n = w_ada.shape[1]
    return pl.pallas_call(
        _ada_kernel,
        grid=(n // d,),
        in_specs=[pl.BlockSpec((b, d), lambda j: (0, 0)),
                  pl.BlockSpec((d, d), lambda j: (0, j)),
                  pl.BlockSpec((1, d), lambda j: (0, j))],
        out_specs=pl.BlockSpec((b, d), lambda j: (0, j)),
        out_shape=jax.ShapeDtypeStruct((b, n), F32),
        compiler_params=_params(("arbitrary",)),
        name="ada_mod",
    )(c, w_ada, b_ada.reshape(1, n))


def _rope(t, cos, sin_a, sin_b, shift):
    return (t * cos + pltpu.roll(t, LANES - shift, 1) * sin_a + pltpu.roll(t, shift, 1) * sin_b)


def _inproj_kernel(x_ref, pos_ref, mod_ref, g1_ref, win_ref, wdvt_ref, gq_ref, wuq_ref, gkv_ref,
                   wuk_ref, wuvt_ref, freq_ref, mask_ref,
                   qm_ref, km_ref, vmt_ref, dq_ref, dk_ref, dvt_ref):
    x = x_ref[0]
    mod = mod_ref[0]
    h = (_rms(x, g1_ref[...], 1e-6) * (1.0 + mod[1:2]) + mod[0:1]).astype(BF16)
    proj = jnp.dot(h, win_ref[...], preferred_element_type=F32)
    dvt_ref[0] = _nt(wdvt_ref[...], h).astype(BF16)

    ang = pos_ref[0].astype(F32) * freq_ref[...]
    cos, sin = jnp.cos(ang), jnp.sin(ang)
    mk = mask_ref[...]
    cos_m = jnp.where(mk[0:1] > 0, cos, 1.0)
    sa_m, sb_m = sin * mk[1:2], sin * mk[2:3]
    cos_d = jnp.where(mk[3:4] > 0, cos, 1.0)
    sa_d, sb_d = sin * mk[4:5], sin * mk[5:6]
    mla_scale = MLA_QK ** -0.5 * LOG2E
    diff_scale = DIFF_DIM ** -0.5 * LOG2E

    cq = _rms(proj[:, C_Q:C_Q + Q_RANK], gq_ref[...], 1e-6)
    q = jnp.dot(cq.astype(BF16), wuq_ref[...], preferred_element_type=F32)
    ckv = _rms(proj[:, C_KV:C_KV + KV_RANK], gkv_ref[...], 1e-6).astype(BF16)
    kn = jnp.dot(ckv, wuk_ref[...], preferred_element_type=F32)
    vmt_ref[0] = _nt(wuvt_ref[...], ckv).astype(BF16)
    kpe = _rope(proj[:, C_KPE:C_KPE + LANES], cos_m, sa_m, sb_m, MLA_HALF)
    cq_m, saq_m, sbq_m = cos_m * mla_scale, sa_m * mla_scale, sb_m * mla_scale
    for hd in range(MLA_HEADS):
        sl = slice(hd * HEAD_PAD, (hd + 1) * HEAD_PAD)
        qm_ref[0, :, sl] = _rope(q[:, sl], cq_m, saq_m, sbq_m, MLA_HALF).astype(BF16)
        km_ref[0, :, sl] = (kn[:, sl] + kpe).astype(BF16)

    cq_d, saq_d, sbq_d = cos_d * diff_scale, sa_d * diff_scale, sb_d * diff_scale
    for hd in range(DIFF_HEADS):
        sl = slice(hd * LANES, (hd + 1) * LANES)
        tq = proj[:, C_DQ + hd * LANES:C_DQ + (hd + 1) * LANES]
        tk = proj[:, C_DK + hd * LANES:C_DK + (hd + 1) * LANES]
        dq_ref[0, :, sl] = _rope(tq, cq_d, saq_d, sbq_d, DIFF_HALF).astype(BF16)
        dk_ref[0, :, sl] = _rope(tk, cos_d, sa_d, sb_d, DIFF_HALF).astype(BF16)


def _rope_tables():
    lane = jnp.arange(LANES)
    f_m = ROPE_THETA ** (-jnp.arange(MLA_HALF, dtype=F32) / MLA_HALF)
    f_d = ROPE_THETA ** (-jnp.arange(DIFF_HALF, dtype=F32) / DIFF_HALF)
    m_x1 = (lane >= MLA_X1) & (lane < MLA_X1 + MLA_HALF)
    m_x2 = (lane >= MLA_X1 + MLA_HALF) & (lane < MLA_X1 + 2 * MLA_HALF)
    d_x1 = (lane % DIFF_DIM) < DIFF_HALF
    d_x2 = ((lane % DIFF_DIM) >= DIFF_HALF) & ((lane % DIFF_DIM) < 2 * DIFF_HALF)
    freq = jnp.where(m_x1 | m_x2, f_m[(lane - MLA_X1) % MLA_HALF], 0.0)
    freq = jnp.where(d_x1 | d_x2, f_d[lane % DIFF_HALF], freq)
    z = jnp.zeros((LANES,), F32)
    masks = jnp.stack([(m_x1 | m_x2).astype(F32), -m_x1.astype(F32), m_x2.astype(F32),
                       (d_x1 | d_x2).astype(F32), -d_x1.astype(F32), d_x2.astype(F32), z, z])
    return freq.reshape(1, LANES).astype(F32), masks


def _pad_in_weights(w_in, w_uq, w_ukv):
    d = w_in.shape[0]
    c_kpe = Q_RANK + KV_RANK
    c_dv = c_kpe + MLA_ROPE + 2 * DIFF_WIDTH
    kpe = jnp.concatenate([jnp.zeros((d, MLA_X1), F32), w_in[:, c_kpe:c_kpe + MLA_ROPE]], axis=1)
    w_in_p = jnp.concatenate([w_in[:, :c_kpe], kpe, w_in[:, c_kpe + MLA_ROPE:c_dv]], axis=1)
    w_dvt = w_in[:, c_dv:].T
    uq = w_uq.reshape(Q_RANK, MLA_HEADS, MLA_QK)
    uq_p = jnp.concatenate([uq[:, :, :MLA_NOPE],
                            jnp.zeros((Q_RANK, MLA_HEADS, MLA_X1 - MLA_NOPE), F32),
                            uq[:, :, MLA_NOPE:]], axis=2).reshape(Q_RANK, MLA_HEADS * HEAD_PAD)
    ukv = w_ukv.reshape(KV_RANK, MLA_HEADS, MLA_NOPE + MLA_V)
    uk_p = jnp.concatenate([ukv[:, :, :MLA_NOPE],
                            jnp.zeros((KV_RANK, MLA_HEADS, HEAD_PAD - MLA_NOPE), F32)],
                           axis=2).reshape(KV_RANK, MLA_HEADS * HEAD_PAD)
    uvt = ukv[:, :, MLA_NOPE:].reshape(KV_RANK, MLA_HEADS * MLA_V).T
    return (w_in_p.astype(BF16), w_dvt.astype(BF16), uq_p.astype(BF16), uk_p.astype(BF16),
            uvt.astype(BF16))


def _in_projection(x, b0, positions, mod, g_norm1, w_in_p, w_dvt, g_q, w_uq_p, g_kv, w_uk_p, w_uvt):
    _, s, d = x.shape
    b = positions.shape[0]
    tm = min(TOKEN_TILE, s)
    freq, masks = _rope_tables()
    full = lambda a: pl.BlockSpec(a.shape, lambda i, j: (0,) * a.ndim)
    tile = lambda w: pl.BlockSpec((1, tm, w), lambda i, j: (i, j, 0))
    x_tile = pl.BlockSpec((1, tm, d), lambda i, j: (b0 + i, j, 0))
    tile_t = lambda w: pl.BlockSpec((1, w, tm), lambda i, j: (i, 0, j))
    wide, vm_w = MLA_HEADS * HEAD_PAD, MLA_HEADS * MLA_V
    g1, gq, gkv = g_norm1.reshape(1, d), g_q.reshape(1, Q_RANK), g_kv.reshape(1, KV_RANK)
    tok = lambda w: jax.ShapeDtypeStruct((b, s, w), BF16)
    feat = lambda w: jax.ShapeDtypeStruct((b, w, s), BF16)
    return pl.pallas_call(
        _inproj_kernel,
        grid=(b, s // tm),
        in_specs=[x_tile, tile(1), pl.BlockSpec((1, N_MOD, d), lambda i, j: (i, 0, 0)),
                  full(g1), full(w_in_p), full(w_dvt), full(gq), full(w_uq_p), full(gkv),
                  full(w_uk_p), full(w_uvt), full(freq), full(masks)],
        out_specs=[tile(wide), tile(wide), tile_t(vm_w), tile(DIFF_WIDTH), tile(DIFF_WIDTH),
                   tile_t(DIFF_WIDTH)],
        out_shape=[tok(wide), tok(wide), feat(vm_w), tok(DIFF_WIDTH), tok(DIFF_WIDTH),
                   feat(DIFF_WIDTH)],
        compiler_params=_params(("parallel", "parallel")),
        name="in_projection",
    )(x, positions.reshape(b, s, 1), mod, g1, w_in_p, w_dvt, gq, w_uq_p, gkv, w_uk_p, w_uvt,
      freq, masks)


def _diag_mask(t):
    key = lax.broadcasted_iota(jnp.int32, (t, t), 0) // CHUNK
    qry = lax.broadcasted_iota(jnp.int32, (t, t), 1) // CHUNK
    return key <= qry


def _scores_t(q, k_ref, lanes, q0, t):
    diag = jnp.where(_diag_mask(t), _nt(k_ref[0, q0:q0 + t, lanes], q), NEG_BIG)
    bulk = _nt(k_ref[0, 0:q0, lanes], q) if q0 else None
    return bulk, diag


def _softmax_t(bulk, diag):
    m = jnp.max(diag, axis=0, keepdims=True)
    if bulk is not None:
        m = jnp.maximum(m, jnp.max(bulk, axis=0, keepdims=True))
    p_diag = jnp.exp2(diag - m)
    l = jnp.sum(p_diag, axis=0, keepdims=True)
    p_bulk = None
    if bulk is not None:
        p_bulk = jnp.exp2(bulk - m)
        l = l + jnp.sum(p_bulk, axis=0, keepdims=True)
    return p_bulk, p_diag, l


def _pv_t(vt_ref, rows, p_bulk, p_diag, q0, t):
    o = jnp.dot(vt_ref[0, rows, q0:q0 + t], p_diag.astype(BF16), preferred_element_type=F32)
    if p_bulk is not None:
        o = o + jnp.dot(vt_ref[0, rows, 0:q0], p_bulk.astype(BF16), preferred_element_type=F32)
    return o


def _mla_attn_kernel(q_ref, k_ref, vt_ref, o_ref, *, seq, t):
    units = [(q0, i) for q0 in range(0, seq, t) for i in range(2)]

    def scores(q0, i):
        lanes = slice(i * HEAD_PAD, (i + 1) * HEAD_PAD)
        return _scores_t(q_ref[0, q0:q0 + t, lanes], k_ref, lanes, q0, t)

    ahead = scores(*units[0])
    outs = []
    for u, (q0, i) in enumerate(units):
        bulk, diag = ahead
        ahead = scores(*units[u + 1]) if u + 1 < len(units) else None
        p_bulk, p_diag, l = _softmax_t(bulk, diag)
        rows = slice(i * MLA_V, (i + 1) * MLA_V)
        outs.append(_pv_t(vt_ref, rows, p_bulk, p_diag, q0, t) / l)
        if i == 1:
            o_ref[0, q0:q0 + t, :] = jnp.concatenate(outs, axis=0).T.astype(BF16)
            outs = []


def _diff_attn_kernel(lam_ref, gs_ref, q_ref, k_ref, vt_ref, o_ref, *, seq, t):
    lv = lam_ref[...]
    lam = (jnp.exp(jnp.sum(lv[0:1] * lv[1:2], axis=-1, keepdims=True))
           - jnp.exp(jnp.sum(lv[2:3] * lv[3:4], axis=-1, keepdims=True)) + LAMBDA_INIT)
    lane = lax.broadcasted_iota(jnp.int32, (t, 2 * DIFF_DIM), 1)
    every = slice(0, 2 * DIFF_DIM)
    def scores(q0):
        q = q_ref[0, q0:q0 + t, :]
        zero = jnp.zeros_like(q)
        return (_scores_t(jnp.where(lane < DIFF_DIM, q, zero), k_ref, every, q0, t),
                _scores_t(jnp.where(lane >= DIFF_DIM, q, zero), k_ref, every, q0, t))

    ahead = scores(0)
    for q0 in range(0, seq, t):
        (b1, d1), (b2, d2) = ahead
        ahead = scores(q0 + t) if q0 + t < seq else None
        pb1, pd1, l1 = _softmax_t(b1, d1)
        pb2, pd2, l2 = _softmax_t(b2, d2)
        c1, c2 = 1.0 / l1, lam / l2
        w_diag = pd1 * c1 - pd2 * c2
        w_bulk = (pb1 * c1 - pb2 * c2) if q0 else None
        o = _pv_t(vt_ref, every, w_bulk, w_diag, q0, t)
        o = (o * lax.rsqrt(jnp.mean(o * o, axis=0, keepdims=True) + 1e-5) * gs_ref[...]
             * (1.0 - LAMBDA_INIT))
        o_ref[0, q0:q0 + t, :] = o.T.astype(BF16)


def _mla_attention(qm, km, vmt):
    b, s, _ = qm.shape
    t = min(MLA_TILE, s)
    pair = 2 * HEAD_PAD
    return pl.pallas_call(
        functools.partial(_mla_attn_kernel, seq=s, t=t),
        grid=(b, MLA_HEADS // 2),
        in_specs=[pl.BlockSpec((1, s, pair), lambda i, j: (i, 0, j)),
                  pl.BlockSpec((1, s, pair), lambda i, j: (i, 0, j)),
                  pl.BlockSpec((1, LANES, s), lambda i, j: (i, j, 0))],
        out_specs=pl.BlockSpec((1, s, LANES), lambda i, j: (i, 0, j)),
        out_shape=jax.ShapeDtypeStruct((b, s, MLA_HEADS * MLA_V), BF16),
        compiler_params=_params(("parallel", "parallel")),
        name="mla_attention",
    )(qm, km, vmt)


def _diff_attention(lam_vecs, g_subln, dq, dk, dvt):
    b, s, _ = dq.shape
    t = min(DIFF_TILE, s)
    blk = pl.BlockSpec((1, s, LANES), lambda i, j: (i, 0, j))
    gs = g_subln.reshape(2 * DIFF_DIM, 1)
    return pl.pallas_call(
        functools.partial(_diff_attn_kernel, seq=s, t=t),
        grid=(b, DIFF_HEADS),
        in_specs=[pl.BlockSpec(lam_vecs.shape, lambda i, j: (0, 0)),
                  pl.BlockSpec(gs.shape, lambda i, j: (0, 0)), blk, blk,
                  pl.BlockSpec((1, LANES, s), lambda i, j: (i, j, 0))],
        out_specs=blk,
        out_shape=jax.ShapeDtypeStruct((b, s, DIFF_WIDTH), BF16),
        compiler_params=_params(("parallel", "parallel")),
        name="diff_attention",
    )(lam_vecs, gs, dq, dk, dvt)


def _outproj_kernel(om_ref, od_ref, x_ref, mod_ref, wout_ref, g2_ref, wr_ref, br_ref,
                    x1_ref, h2_ref, idx_ref, tw_ref, rank_ref, cnt_ref, carry_ref):
    first = (pl.program_id(0) == 0) & (pl.program_id(1) == 0)

    @pl.when(first)
    def _():
        carry_ref[...] = jnp.zeros(carry_ref.shape, F32)

    half = om_ref.shape[-1]
    mix = (jnp.dot(om_ref[0], wout_ref[:half, :], preferred_element_type=F32)
           + jnp.dot(od_ref[0], wout_ref[half:, :], preferred_element_type=F32))
    mod = mod_ref[0]
    x1 = x_ref[0] + mod[2:3] * mix
    h2 = _rms(x1, g2_ref[...], 1e-6) * (1.0 + mod[4:5]) + mod[3:4]
    x1_ref[0] = x1
    packed = _pack_pairs(h2)
    for c in range(h2_ref.shape[0]):
        h2_ref[c, 0] = packed[:, c * STRIP:(c + 1) * STRIP]

    logits = lax.dot_general(wr_ref[...], h2, (((1,), (1,)), ((), ())),
                             preferred_element_type=F32, precision=HIGHEST) + br_ref[...]
    n_e, t = logits.shape
    e_iota = lax.broadcasted_iota(jnp.int32, (n_e, t), 0)
    vals, idxs = [], []
    rest = logits
    for _ in range(TOP_K):
        m = jnp.max(rest, axis=0, keepdims=True)
        ik = jnp.min(jnp.where(rest == m, e_iota, n_e), axis=0, keepdims=True)
        vals.append(m)
        idxs.append(ik)
        rest = jnp.where(e_iota == ik, -jnp.inf, rest)
    ex = [jnp.exp(v - vals[0]) for v in vals]
    den = ex[0] + ex[1] + ex[2] + ex[3]
    sel = (e_iota == idxs[0]) | (e_iota == idxs[1]) | (e_iota == idxs[2]) | (e_iota == idxs[3])

    before = (lax.broadcasted_iota(jnp.int32, (t, t), 0)
              < lax.broadcasted_iota(jnp.int32, (t, t), 1)).astype(BF16)
    prefix = jnp.dot(sel.astype(BF16), before, preferred_element_type=F32)
    pos = carry_ref[...] + prefix
    for k in range(TOP_K):
        idx_ref[0, k:k + 1, :] = idxs[k]
        tw_ref[0, k:k + 1, :] = ex[k] / den
        rk = jnp.sum(jnp.where(e_iota == idxs[k], pos, 0.0), axis=0, keepdims=True)
        rank_ref[0, k:k + 1, :] = rk.astype(jnp.int32)
    total = carry_ref[...] + jnp.sum(sel.astype(F32), axis=1, keepdims=True)
    carry_ref[...] = total
    cnt_ref[...] = total.astype(jnp.int32)


def _out_projection(o_mla, o_diff, x, b0, mod, w_out, g_norm2, w_router, b_router):
    _, s, d = x.shape
    b = o_mla.shape[0]
    tm = min(TOKEN_TILE, s)
    nt = s // tm
    half = o_mla.shape[-1]
    full = lambda a: pl.BlockSpec(a.shape, lambda i, j: (0,) * a.ndim)
    tile = lambda w: pl.BlockSpec((1, tm, w), lambda i, j: (i, j, 0))
    x_tile = pl.BlockSpec((1, tm, d), lambda i, j: (b0 + i, j, 0))
    route = pl.BlockSpec((1, TOP_K, tm), lambda i, j: (i * nt + j, 0, 0))
    g2 = g_norm2.reshape(1, d)
    wr = w_router.T
    br = b_router.reshape(N_EXPERTS, 1)
    w_out_b = w_out.astype(BF16)
    n_strips = d // (2 * STRIP)
    return pl.pallas_call(
        _outproj_kernel,
        grid=(b, nt),
        in_specs=[tile(half), tile(half), x_tile, pl.BlockSpec((1, N_MOD, d), lambda i, j: (i, 0, 0)),
                  full(w_out_b), full(g2), full(wr), full(br)],
        out_specs=[tile(d), pl.BlockSpec((n_strips, 1, tm, STRIP), lambda i, j: (0, i, j, 0)),
                   route, route, route,
                   pl.BlockSpec((N_EXPERTS, 1), lambda i, j: (0, 0))],
        out_shape=[jax.ShapeDtypeStruct((b, s, d), F32),
                   jax.ShapeDtypeStruct((n_strips, b, s, STRIP), jnp.uint32),
                   jax.ShapeDtypeStruct((b * nt, TOP_K, tm), jnp.int32),
                   jax.ShapeDtypeStruct((b * nt, TOP_K, tm), F32),
                   jax.ShapeDtypeStruct((b * nt, TOP_K, tm), jnp.int32),
                   jax.ShapeDtypeStruct((N_EXPERTS, 1), jnp.int32)],
        scratch_shapes=[pltpu.VMEM((N_EXPERTS, 1), F32)],
        compiler_params=_params(("arbitrary", "arbitrary")),
        name="out_projection_router",
    )(o_mla, o_diff, x, mod, w_out_b, g2, wr, br)


def _split_gate_up_perm():
    r = lax.broadcasted_iota(jnp.int32, (MXU_COLS, MXU_COLS), 0)
    c = lax.broadcasted_iota(jnp.int32, (MXU_COLS, MXU_COLS), 1)
    src = jnp.where(c < LANES, 2 * c, 2 * (c - LANES) + 1)
    return (r == src).astype(BF16)


def _expert_kernel(ie_ref, ib_ref, lo_ref, hi_ref, nv_ref, xs_ref, wgu_ref, bgu_ref, wd_ref, bd_ref,
                   y_ref, wgu_s, wd_s, y_s):
    i = pl.program_id(0)
    prev = jnp.maximum(i - 1, 0)
    valid = i < nv_ref[0]
    new_expert = (i == 0) | (ie_ref[i] != ie_ref[prev])
    new_block = (i == 0) | (ib_ref[i] != ib_ref[prev])
    n_groups = wgu_s.shape[1] // MXU_COLS
    n_strips = y_ref.shape[0]
    row = lax.broadcasted_iota(jnp.int32, (y_ref.shape[1], 1), 0)
    mine = (row >= lo_ref[i]) & (row < hi_ref[i])

    @pl.when(valid & new_expert)
    def _():
        perm = _split_gate_up_perm()
        for c in range(n_groups):
            cols = slice(c * MXU_COLS, (c + 1) * MXU_COLS)
            w = wgu_ref[0, :, cols].astype(BF16)
            wgu_s[:, cols] = jnp.dot(w, perm, preferred_element_type=F32).astype(BF16)
        wd_s[...] = wd_ref[0].astype(BF16)

    @pl.when(valid)
    def _():
        lo, hi = _unpack_pairs(jnp.concatenate([xs_ref[c] for c in range(n_strips)], axis=1))
        xb = jnp.concatenate([lo, hi], axis=1).astype(BF16)
        gu = jnp.dot(xb, wgu_s[...], preferred_element_type=F32) + bgu_ref[0]
        acts = []
        for c in range(n_groups):
            g = jnp.minimum(gu[:, c * MXU_COLS:c * MXU_COLS + LANES], SWIGLU_LIMIT)
            u = jnp.clip(gu[:, c * MXU_COLS + LANES:(c + 1) * MXU_COLS], -SWIGLU_LIMIT, SWIGLU_LIMIT)
            acts.append((g / (1.0 + jnp.exp(-SWIGLU_ALPHA * g)) * (u + 1.0)).astype(BF16))
        act = jnp.concatenate(acts, axis=1)
        y_s[...] = _pack_pairs(jnp.dot(act, wd_s[...], preferred_element_type=F32) + bd_ref[0])

    @pl.when(valid & new_block)
    def _():
        for c in range(n_strips):
            y_ref[c] = jnp.where(mine, y_s[:, c * STRIP:(c + 1) * STRIP], jnp.uint32(0))

    @pl.when(valid & jnp.logical_not(new_block))
    def _():
        for c in range(n_strips):
            y_ref[c] = jnp.where(mine, y_s[:, c * STRIP:(c + 1) * STRIP], y_ref[c])


def _expert_ffn(items, xs, w_gate_up, b_gate_up_grouped, w_down, b_down):
    n_strips, a, _ = xs.shape
    d = 2 * n_strips * STRIP
    dff2 = w_gate_up.shape[-1]
    dff = dff2 // 2
    n_items = items[0].shape[0]
    wmap = lambda i, ie, ib, lo, hi, nv: (ie[i], 0, 0)
    rmap = lambda i, ie, ib, lo, hi, nv: (0, ib[i], 0)
    grid_spec = pltpu.PrefetchScalarGridSpec(
        num_scalar_prefetch=5,
        grid=(n_items,),
        in_specs=[pl.BlockSpec((n_strips, MOE_ROWS, STRIP), rmap),
                  pl.BlockSpec((1, d, dff2), wmap), pl.BlockSpec((1, 1, dff2), wmap),
                  pl.BlockSpec((1, dff, d), wmap), pl.BlockSpec((1, 1, d), wmap)],
        out_specs=pl.BlockSpec((n_strips, MOE_ROWS, STRIP), rmap),
        scratch_shapes=[pltpu.VMEM((d, dff2), BF16), pltpu.VMEM((dff, d), BF16),
                        pltpu.VMEM((MOE_ROWS, d // 2), jnp.uint32)],
    )
    return pl.pallas_call(
        _expert_kernel,
        grid_spec=grid_spec,
        out_shape=jax.ShapeDtypeStruct((n_strips, a, STRIP), jnp.uint32),
        compiler_params=_params(("arbitrary",), EXPERT_VMEM_LIMIT),
        name="expert_ffn",
    )(*items, xs, w_gate_up, b_gate_up_grouped, w_down, b_down)


def _expert_work_items(counts, n_rows):
    e_ids = jnp.arange(N_EXPERTS, dtype=jnp.int32)
    ends = jnp.cumsum(counts).astype(jnp.int32)
    starts = ends - counts
    first_blk = starts // MOE_ROWS
    last_blk = (ends - 1) // MOE_ROWS
    per_expert = jnp.where(counts > 0, last_blk - first_blk + 1, 0)
    item_end = jnp.cumsum(per_expert).astype(jnp.int32)
    item_start = item_end - per_expert
    total = item_end[-1:]
    max_items = n_rows // MOE_ROWS + N_EXPERTS - 1
    it = jnp.clip(jnp.arange(max_items, dtype=jnp.int32), 0, jnp.maximum(total - 1, 0))
    expert = jnp.sum(item_end[None, :] <= it[:, None], axis=1).astype(jnp.int32)
    pick = lambda v: jnp.sum(jnp.where(expert[:, None] == e_ids, v, 0), axis=1)
    block = pick(first_blk) + it - pick(item_start)
    lo = jnp.maximum(pick(starts) - block * MOE_ROWS, 0)
    hi = jnp.minimum(pick(ends) - block * MOE_ROWS, MOE_ROWS)
    return expert, block, lo, hi, total, starts


def _gather_strips(strips, idx):
    c, r, w = strips.shape
    offsets = jnp.arange(c, dtype=jnp.int32)[:, None] * r
    out = _gather_rows(strips.reshape(c * r, w), (idx[None, :] + offsets).reshape(-1))
    return out.reshape(c, idx.shape[0], w)


def _gather_rows(table, idx):
    m = idx.shape[0]
    w = table.shape[1]
    window = GATHER_WINDOW
    mesh = plsc.VectorSubcoreMesh(core_axis_name="core", subcore_axis_name="subcore")

    @pl.kernel(out_type=jax.ShapeDtypeStruct((m, w), table.dtype), mesh=mesh)
    def gather_kernel(table_hbm, idx_hbm, out_hbm):
        def body(idx_vmem, out_vmem):
            pltpu.sync_copy(table_hbm.at[idx_vmem.at[0]], out_vmem)

        pltpu.emit_pipeline(
            body,
            grid=(m // window,),
            in_specs=[pl.BlockSpec((1, window), index_map=lambda i: (0, i))],
            out_specs=[pl.BlockSpec((window, w), index_map=lambda i: (i, 0))],
            core_axis_name=("core", "subcore"),
            dimension_semantics=(pltpu.PARALLEL,),
        )(idx_hbm, out_hbm)

    return gather_kernel(table, idx.reshape(1, m))


def _combine_kernel(yk_ref, tw_ref, x1_ref, mod_ref, gf_ref, *rest):
    o_ref = rest[-1]
    tw = tw_ref[0]
    lows, highs = [], []
    for c in range(yk_ref.shape[0]):
        lo, hi = _unpack_pairs(yk_ref[c, 0, 0])
        lo, hi = lo * tw[:, 0:1], hi * tw[:, 0:1]
        for k in range(1, TOP_K):
            lo_k, hi_k = _unpack_pairs(yk_ref[c, k, 0])
            lo, hi = lo + lo_k * tw[:, k:k + 1], hi + hi_k * tw[:, k:k + 1]
        lows.append(lo)
        highs.append(hi)
    y = jnp.concatenate(lows + highs, axis=1)
    x2 = x1_ref[0] + mod_ref[0][5:6] * y
    o_ref[0] = _rms(x2, gf_ref[...], 1e-6)


def _combine(yk, tw, x1, mod, g_final, b0, b_total, out_so_far):
    b, s, d = x1.shape
    tm = min(TOKEN_TILE, s)
    gf = g_final.reshape(1, d)
    in_specs = [pl.BlockSpec((d // (2 * STRIP), TOP_K, 1, tm, STRIP), lambda i, j: (0, 0, i, j, 0)),
                pl.BlockSpec((1, tm, TOP_K), lambda i, j: (i, j, 0)),
                pl.BlockSpec((1, tm, d), lambda i, j: (i, j, 0)),
                pl.BlockSpec((1, N_MOD, d), lambda i, j: (i, 0, 0)),
                pl.BlockSpec((1, d), lambda i, j: (0, 0))]
    operands = [yk, tw, x1, mod, gf]
    aliases = {}
    if out_so_far is not None:
        in_specs.append(pl.BlockSpec(memory_space=pl.ANY))
        operands.append(out_so_far)
        aliases = {len(operands) - 1: 0}
    return pl.pallas_call(
        _combine_kernel,
        grid=(b, s // tm),
        in_specs=in_specs,
        out_specs=pl.BlockSpec((1, tm, d), lambda i, j: (b0 + i, j, 0)),
        out_shape=jax.ShapeDtypeStruct((b_total, s, d), F32),
        input_output_aliases=aliases,
        compiler_params=_params(("parallel", "parallel")),
        name="combine_final_norm",
    )(*operands)


def kernel(x, c, positions, w_ada, b_ada, g_norm1, w_in, g_q_norm, w_uq, g_kv_norm, w_ukv,
           lambda_q1, lambda_k1, lambda_q2, lambda_k2, g_subln, w_out, g_norm2,
           w_router, b_router, w_gate_up, b_gate_up, w_down, b_down, g_final):
    b_total, s, d = x.shape
    l = 0
    mod_all = _ada_mod(c, w_ada[l], b_ada[l]).reshape(b_total, N_MOD, d)
    w_in_p, w_dvt, w_uq_p, w_uk_p, w_uvt = _pad_in_weights(w_in[l], w_uq[l], w_ukv[l])
    lam_vecs = jnp.stack([lambda_q1[l], lambda_k1[l], lambda_q2[l], lambda_k2[l]]).astype(F32)
    bgu = (b_gate_up[l].reshape(N_EXPERTS, EXPERT_DFF // LANES, LANES, 2)
           .transpose(0, 1, 3, 2).reshape(N_EXPERTS, 1, 2 * EXPERT_DFF))
    bd = b_down[l].reshape(N_EXPERTS, 1, d)

    groups = BATCH_GROUPS if b_total % BATCH_GROUPS == 0 else 1
    b = b_total // groups
    n = b * s
    out = None
    for g in range(groups):
        b0 = g * b
        mod = mod_all[b0:b0 + b]
        qm, km, vmt, dq, dk, dvt = _in_projection(x, b0, positions[b0:b0 + b], mod, g_norm1[l],
                                                  w_in_p, w_dvt, g_q_norm[l], w_uq_p,
                                                  g_kv_norm[l], w_uk_p, w_uvt)
        o_mla = _mla_attention(qm, km, vmt)
        o_diff = _diff_attention(lam_vecs, g_subln[l], dq, dk, dvt)
        x1, h2, idx, tw, rank, cnt = _out_projection(o_mla, o_diff, x, b0, mod, w_out[l],
                                                     g_norm2[l], w_router[l], b_router[l])

        to_kn = lambda a: a.transpose(1, 0, 2).reshape(TOP_K, n)
        idx, tw, rank = to_kn(idx), to_kn(tw), to_kn(rank)
        *items, starts = _expert_work_items(cnt[:, 0], n * TOP_K)
        e_ids = jnp.arange(N_EXPERTS, dtype=jnp.int32)
        slot = rank + jnp.sum(jnp.where(idx[..., None] == e_ids, starts, 0), axis=-1)
        tok = jnp.arange(n, dtype=jnp.int32)
        row_tok = jnp.sort((idx * n + tok).reshape(-1)) % n

        xs = _gather_strips(h2.reshape(-1, n, STRIP), row_tok)
        y = _expert_ffn(items, xs, w_gate_up[l], bgu, w_down[l], bd)
        yk = _gather_strips(y, slot.reshape(-1)).reshape(-1, TOP_K, b, s, STRIP)
        out = _combine(yk, tw.T.reshape(b, s, TOP_K), x1, mod, g_final, b0, b_total, out)
    return out
```

```python
import functools
import math

import jax
import jax.numpy as jnp
from jax import lax
from jax.experimental import pallas as pl
from jax.experimental.pallas import tpu as pltpu
from jax.experimental.pallas import tpu_sc as plsc

F32 = jnp.float32
BF16 = jnp.bfloat16
HIGHEST = lax.Precision.HIGHEST

D_MODEL = 1024
CHUNK = 64
ROPE_THETA = 500000.0
MLA_HEADS = 8
MLA_NOPE = 64
MLA_ROPE = 32
MLA_V = 64
MLA_QK = MLA_NOPE + MLA_ROPE
Q_RANK = 384
KV_RANK = 256
DIFF_HEADS = 4
DIFF_DIM = 64
DIFF_ROPE = DIFF_DIM // 4
DIFF_WIDTH = DIFF_HEADS * 2 * DIFF_DIM
N_EXPERTS = 32
TOP_K = 4
EXPERT_DFF = D_MODEL
SWIGLU_LIMIT = 7.0
SWIGLU_ALPHA = 1.702
N_MOD = 6
LAMBDA_INIT = 0.8 - 0.6 * math.exp(-0.3 * 0)
LOG2E = math.log2(math.e)

LANES = 128
MXU_COLS = 256
HEAD_PAD = LANES
TOKEN_TILE = 512
MLA_TILE = 512
DIFF_TILE = 256
BATCH_GROUPS = 1
ROW_GROUPS = 2
MOE_ROWS = 512
GATHER_WINDOW = 128
STRIP = 256
VMEM_LIMIT = 48 * 1024 * 1024
EXPERT_VMEM_LIMIT = 58 * 1024 * 1024
NEG_BIG = -1e30

MLA_X1 = 96
MLA_HALF = MLA_ROPE // 2
DIFF_HALF = DIFF_ROPE // 2

C_Q = 0
C_KV = Q_RANK
C_KPE = Q_RANK + KV_RANK
C_DQ = C_KPE + LANES
C_DK = C_DQ + DIFF_WIDTH
IN_COLS_PAD = C_DK + DIFF_WIDTH


def _rms(x, g, eps):
    return x * lax.rsqrt(jnp.mean(x * x, axis=-1, keepdims=True) + eps) * g


def _pack_pairs(x):
    w = x.shape[1] // 2
    as_bits = lambda v: lax.bitcast_convert_type(v.astype(BF16).astype(F32), jnp.uint32)
    return (as_bits(x[:, w:]) & jnp.uint32(0xFFFF0000)) | (as_bits(x[:, :w]) >> 16)


def _unpack_pairs(p):
    lo = lax.bitcast_convert_type(p << 16, F32)
    hi = lax.bitcast_convert_type(p & jnp.uint32(0xFFFF0000), F32)
    return lo, hi


def _params(sem, limit=VMEM_LIMIT):
    return pltpu.CompilerParams(dimension_semantics=sem, vmem_limit_bytes=limit)


def _nt(a, b):
    return lax.dot_general(a, b, (((1,), (1,)), ((), ())), preferred_element_type=F32)


def _ada_kernel(c_ref, w_ref, b_ref, o_ref):
    c = c_ref[...]
    sc = c / (1.0 + jnp.exp(-c))
    o_ref[...] = jnp.dot(sc, w_ref[...], preferred_element_type=F32, precision=HIGHEST) + b_ref[...]


def _ada_mod(c, w_ada, b_ada):
    b, d = c.shape
    n = w_ada.shape[1]
    return pl.pallas_call(
        _ada_kernel,
        grid=(n // d,),
        in_specs=[pl.BlockSpec((b, d), lambda j: (0, 0)),
                  pl.BlockSpec((d, d), lambda j: (0, j)),
                  pl.BlockSpec((1, d), lambda j: (0, j))],
        out_specs=pl.BlockSpec((b, d), lambda j: (0, j)),
        out_shape=jax.ShapeDtypeStruct((b, n), F32),
        compiler_params=_params(("arbitrary",)),
        name="ada_mod",
    )(c, w_ada, b_ada.reshape(1, n))


def _rope(t, cos, sin_a, sin_b, shift):
    return (t * cos + pltpu.roll(t, LANES - shift, 1) * sin_a + pltpu.roll(t, shift, 1) * sin_b)


def _inproj_kernel(x_ref, pos_ref, mod_ref, g1_ref, win_ref, wdvt_ref, gq_ref, wuq_ref, gkv_ref,
                   wuk_ref, wuvt_ref, freq_ref, mask_ref,
                   qm_ref, km_ref, vmt_ref, dq_ref, dk_ref, dvt_ref):
    x = x_ref[0]
    mod = mod_ref[0]
    h = (_rms(x, g1_ref[...], 1e-6) * (1.0 + mod[1:2]) + mod[0:1]).astype(BF16)
    proj = jnp.dot(h, win_ref[...], preferred_element_type=F32)
    dvt_ref[0] = _nt(wdvt_ref[...], h).astype(BF16)

    ang = pos_ref[0].astype(F32) * freq_ref[...]
    cos, sin = jnp.cos(ang), jnp.sin(ang)
    mk = mask_ref[...]
    cos_m = jnp.where(mk[0:1] > 0, cos, 1.0)
    sa_m, sb_m = sin * mk[1:2], sin * mk[2:3]
    cos_d = jnp.where(mk[3:4] > 0, cos, 1.0)
    sa_d, sb_d = sin * mk[4:5], sin * mk[5:6]
    mla_scale = MLA_QK ** -0.5 * LOG2E
    diff_scale = DIFF_DIM ** -0.5 * LOG2E

    cq = _rms(proj[:, C_Q:C_Q + Q_RANK], gq_ref[...], 1e-6)
    q = jnp.dot(cq.astype(BF16), wuq_ref[...], preferred_element_type=F32)
    ckv = _rms(proj[:, C_KV:C_KV + KV_RANK], gkv_ref[...], 1e-6).astype(BF16)
    kn = jnp.dot(ckv, wuk_ref[...], preferred_element_type=F32)
    vmt_ref[0] = _nt(wuvt_ref[...], ckv).astype(BF16)
    kpe = _rope(proj[:, C_KPE:C_KPE + LANES], cos_m, sa_m, sb_m, MLA_HALF)
    cq_m, saq_m, sbq_m = cos_m * mla_scale, sa_m * mla_scale, sb_m * mla_scale
    for hd in range(MLA_HEADS):
        sl = slice(hd * HEAD_PAD, (hd + 1) * HEAD_PAD)
        qm_ref[0, :, sl] = _rope(q[:, sl], cq_m, saq_m, sbq_m, MLA_HALF).astype(BF16)
        km_ref[0, :, sl] = (kn[:, sl] + kpe).astype(BF16)

    cq_d, saq_d, sbq_d = cos_d * diff_scale, sa_d * diff_scale, sb_d * diff_scale
    for hd in range(DIFF_HEADS):
        sl = slice(hd * LANES, (hd + 1) * LANES)
        tq = proj[:, C_DQ + hd * LANES:C_DQ + (hd + 1) * LANES]
        tk = proj[:, C_DK + hd * LANES:C_DK + (hd + 1) * LANES]
        dq_ref[0, :, sl] = _rope(tq, cq_d, saq_d, sbq_d, DIFF_HALF).astype(BF16)
        dk_ref[0, :, sl] = _rope(tk, cos_d, sa_d, sb_d, DIFF_HALF).astype(BF16)


def _rope_tables():
    lane = jnp.arange(LANES)
    f_m = ROPE_THETA ** (-jnp.arange(MLA_HALF, dtype=F32) / MLA_HALF)
    f_d = ROPE_THETA ** (-jnp.arange(DIFF_HALF, dtype=F32) / DIFF_HALF)
    m_x1 = (lane >= MLA_X1) & (lane < MLA_X1 + MLA_HALF)
    m_x2 = (lane >= MLA_X1 + MLA_HALF) & (lane < MLA_X1 + 2 * MLA_HALF)
    d_x1 = (lane % DIFF_DIM) < DIFF_HALF
    d_x2 = ((lane % DIFF_DIM) >= DIFF_HALF) & ((lane % DIFF_DIM) < 2 * DIFF_HALF)
    freq = jnp.where(m_x1 | m_x2, f_m[(lane - MLA_X1) % MLA_HALF], 0.0)
    freq = jnp.where(d_x1 | d_x2, f_d[lane % DIFF_HALF], freq)
    z = jnp.zeros((LANES,), F32)
    masks = jnp.stack([(m_x1 | m_x2).astype(F32), -m_x1.astype(F32), m_x2.astype(F32),
                       (d_x1 | d_x2).astype(F32), -d_x1.astype(F32), d_x2.astype(F32), z, z])
    return freq.reshape(1, LANES).astype(F32), masks


def _pad_in_weights(w_in, w_uq, w_ukv):
    d = w_in.shape[0]
    c_kpe = Q_RANK + KV_RANK
    c_dv = c_kpe + MLA_ROPE + 2 * DIFF_WIDTH
    kpe = jnp.concatenate([jnp.zeros((d, MLA_X1), F32), w_in[:, c_kpe:c_kpe + MLA_ROPE]], axis=1)
    w_in_p = jnp.concatenate([w_in[:, :c_kpe], kpe, w_in[:, c_kpe + MLA_ROPE:c_dv]], axis=1)
    w_dvt = w_in[:, c_dv:].T
    uq = w_uq.reshape(Q_RANK, MLA_HEADS, MLA_QK)
    uq_p = jnp.concatenate([uq[:, :, :MLA_NOPE],
                            jnp.zeros((Q_RANK, MLA_HEADS, MLA_X1 - MLA_NOPE), F32),
                            uq[:, :, MLA_NOPE:]], axis=2).reshape(Q_RANK, MLA_HEADS * HEAD_PAD)
    ukv = w_ukv.reshape(KV_RANK, MLA_HEADS, MLA_NOPE + MLA_V)
    uk_p = jnp.concatenate([ukv[:, :, :MLA_NOPE],
                            jnp.zeros((KV_RANK, MLA_HEADS, HEAD_PAD - MLA_NOPE), F32)],
                           axis=2).reshape(KV_RANK, MLA_HEADS * HEAD_PAD)
    uvt = ukv[:, :, MLA_NOPE:].reshape(KV_RANK, MLA_HEADS * MLA_V).T
    return (w_in_p.astype(BF16), w_dvt.astype(BF16), uq_p.astype(BF16), uk_p.astype(BF16),
            uvt.astype(BF16))


def _in_projection(x, b0, positions, mod, g_norm1, w_in_p, w_dvt, g_q, w_uq_p, g_kv, w_uk_p, w_uvt):
    _, s, d = x.shape
    b = positions.shape[0]
    tm = min(TOKEN_TILE, s)
    freq, masks = _rope_tables()
    full = lambda a: pl.BlockSpec(a.shape, lambda i, j: (0,) * a.ndim)
    tile = lambda w: pl.BlockSpec((1, tm, w), lambda i, j: (i, j, 0))
    x_tile = pl.BlockSpec((1, tm, d), lambda i, j: (b0 + i, j, 0))
    tile_t = lambda w: pl.BlockSpec((1, w, tm), lambda i, j: (i, 0, j))
    wide, vm_w = MLA_HEADS * HEAD_PAD, MLA_HEADS * MLA_V
    g1, gq, gkv = g_norm1.reshape(1, d), g_q.reshape(1, Q_RANK), g_kv.reshape(1, KV_RANK)
    tok = lambda w: jax.ShapeDtypeStruct((b, s, w), BF16)
    feat = lambda w: jax.ShapeDtypeStruct((b, w, s), BF16)
    return pl.pallas_call(
        _inproj_kernel,
        grid=(b, s // tm),
        in_specs=[x_tile, tile(1), pl.BlockSpec((1, N_MOD, d), lambda i, j: (i, 0, 0)),
                  full(g1), full(w_in_p), full(w_dvt), full(gq), full(w_uq_p), full(gkv),
                  full(w_uk_p), full(w_uvt), full(freq), full(masks)],
        out_specs=[tile(wide), tile(wide), tile_t(vm_w), tile(DIFF_WIDTH), tile(DIFF_WIDTH),
                   tile_t(DIFF_WIDTH)],
        out_shape=[tok(wide), tok(wide), feat(vm_w), tok(DIFF_WIDTH), tok(DIFF_WIDTH),
                   feat(DIFF_WIDTH)],
        compiler_params=_params(("parallel", "parallel")),
        name="in_projection",
    )(x, positions.reshape(b, s, 1), mod, g1, w_in_p, w_dvt, gq, w_uq_p, gkv, w_uk_p, w_uvt,
      freq, masks)


def _diag_mask(t):
    key = lax.broadcasted_iota(jnp.int32, (t, t), 0) // CHUNK
    qry = lax.broadcasted_iota(jnp.int32, (t, t), 1) // CHUNK
    return key <= qry


def _scores_t(q, k_ref, lanes, q0, t):
    diag = jnp.where(_diag_mask(t), _nt(k_ref[0, q0:q0 + t, lanes], q), NEG_BIG)
    bulk = _nt(k_ref[0, 0:q0, lanes], q) if q0 else None
    return bulk, diag


def _softmax_t(bulk, diag):
    m = jnp.max(diag, axis=0, keepdims=True)
    if bulk is not None:
        m = jnp.maximum(m, jnp.max(bulk, axis=0, keepdims=True))
    p_diag = jnp.exp2(diag - m)
    l = jnp.sum(p_diag, axis=0, keepdims=True)
    p_bulk = None
    if bulk is not None:
        p_bulk = jnp.exp2(bulk - m)
        l = l + jnp.sum(p_bulk, axis=0, keepdims=True)
    return p_bulk, p_diag, l


def _pv_t(vt_ref, rows, p_bulk, p_diag, q0, t):
    o = jnp.dot(vt_ref[0, rows, q0:q0 + t], p_diag.astype(BF16), preferred_element_type=F32)
    if p_bulk is not None:
        o = o + jnp.dot(vt_ref[0, rows, 0:q0], p_bulk.astype(BF16), preferred_element_type=F32)
    return o


def _mla_attn_kernel(q_ref, k_ref, vt_ref, o_ref, *, seq, t):
    units = [(q0, i) for q0 in range(0, seq, t) for i in range(2)]

    def scores(q0, i):
        lanes = slice(i * HEAD_PAD, (i + 1) * HEAD_PAD)
        return _scores_t(q_ref[0, q0:q0 + t, lanes], k_ref, lanes, q0, t)

    ahead = scores(*units[0])
    outs = []
    for u, (q0, i) in enumerate(units):
        bulk, diag = ahead
        ahead = scores(*units[u + 1]) if u + 1 < len(units) else None
        p_bulk, p_diag, l = _softmax_t(bulk, diag)
        rows = slice(i * MLA_V, (i + 1) * MLA_V)
        outs.append(_pv_t(vt_ref, rows, p_bulk, p_diag, q0, t) / l)
        if i == 1:
            o_ref[0, q0:q0 + t, :] = jnp.concatenate(outs, axis=0).T.astype(BF16)
            outs = []


def _diff_attn_kernel(lam_ref, gs_ref, q_ref, k_ref, vt_ref, o_ref, *, seq, t):
    lv = lam_ref[...]
    lam = (jnp.exp(jnp.sum(lv[0:1] * lv[1:2], axis=-1, keepdims=True))
           - jnp.exp(jnp.sum(lv[2:3] * lv[3:4], axis=-1, keepdims=True)) + LAMBDA_INIT)
    lane = lax.broadcasted_iota(jnp.int32, (t, 2 * DIFF_DIM), 1)
    every = slice(0, 2 * DIFF_DIM)
    def scores(q0):
        q = q_ref[0, q0:q0 + t, :]
        zero = jnp.zeros_like(q)
        return (_scores_t(jnp.where(lane < DIFF_DIM, q, zero), k_ref, every, q0, t),
                _scores_t(jnp.where(lane >= DIFF_DIM, q, zero), k_ref, every, q0, t))

    ahead = scores(0)
    for q0 in range(0, seq, t):
        (b1, d1), (b2, d2) = ahead
        ahead = scores(q0 + t) if q0 + t < seq else None
        pb1, pd1, l1 = _softmax_t(b1, d1)
        pb2, pd2, l2 = _softmax_t(b2, d2)
        c1, c2 = 1.0 / l1, lam / l2
        w_diag = pd1 * c1 - pd2 * c2
        w_bulk = (pb1 * c1 - pb2 * c2) if q0 else None
        o = _pv_t(vt_ref, every, w_bulk, w_diag, q0, t)
        o = (o * lax.rsqrt(jnp.mean(o * o, axis=0, keepdims=True) + 1e-5) * gs_ref[...]
             * (1.0 - LAMBDA_INIT))
        o_ref[0, q0:q0 + t, :] = o.T.astype(BF16)


def _mla_attention(qm, km, vmt):
    b, s, _ = qm.shape
    t = min(MLA_TILE, s)
    pair = 2 * HEAD_PAD
    return pl.pallas_call(
        functools.partial(_mla_attn_kernel, seq=s, t=t),
        grid=(b, MLA_HEADS // 2),
        in_specs=[pl.BlockSpec((1, s, pair), lambda i, j: (i, 0, j)),
                  pl.BlockSpec((1, s, pair), lambda i, j: (i, 0, j)),
                  pl.BlockSpec((1, LANES, s), lambda i, j: (i, j, 0))],
        out_specs=pl.BlockSpec((1, s, LANES), lambda i, j: (i, 0, j)),
        out_shape=jax.ShapeDtypeStruct((b, s, MLA_HEADS * MLA_V), BF16),
        compiler_params=_params(("parallel", "parallel")),
        name="mla_attention",
    )(qm, km, vmt)


def _diff_attention(lam_vecs, g_subln, dq, dk, dvt):
    b, s, _ = dq.shape
    t = min(DIFF_TILE, s)
    blk = pl.BlockSpec((1, s, LANES), lambda i, j: (i, 0, j))
    gs = g_subln.reshape(2 * DIFF_DIM, 1)
    return pl.pallas_call(
        functools.partial(_diff_attn_kernel, seq=s, t=t),
        grid=(b, DIFF_HEADS),
        in_specs=[pl.BlockSpec(lam_vecs.shape, lambda i, j: (0, 0)),
                  pl.BlockSpec(gs.shape, lambda i, j: (0, 0)), blk, blk,
                  pl.BlockSpec((1, LANES, s), lambda i, j: (i, j, 0))],
        out_specs=blk,
        out_shape=jax.ShapeDtypeStruct((b, s, DIFF_WIDTH), BF16),
        compiler_params=_params(("parallel", "parallel")),
        name="diff_attention",
    )(lam_vecs, gs, dq, dk, dvt)


def _outproj_kernel(om_ref, od_ref, x_ref, mod_ref, wout_ref, g2_ref, wr_ref, br_ref,
                    x1_ref, h2_ref, idx_ref, tw_ref, rank_ref, cnt_ref, carry_ref):
    first = (pl.program_id(0) == 0) & (pl.program_id(1) == 0)

    @pl.when(first)
    def _():
        carry_ref[...] = jnp.zeros(carry_ref.shape, F32)

    half = om_ref.shape[-1]
    mix = (jnp.dot(om_ref[0], wout_ref[:half, :], preferred_element_type=F32)
           + jnp.dot(od_ref[0], wout_ref[half:, :], preferred_element_type=F32))
    mod = mod_ref[0]
    x1 = x_ref[0] + mod[2:3] * mix
    h2 = _rms(x1, g2_ref[...], 1e-6) * (1.0 + mod[4:5]) + mod[3:4]
    x1_ref[0] = x1
    packed = _pack_pairs(h2)
    for c in range(h2_ref.shape[0]):
        h2_ref[c, 0] = packed[:, c * STRIP:(c + 1) * STRIP]

    logits = lax.dot_general(wr_ref[...], h2, (((1,), (1,)), ((), ())),
                             preferred_element_type=F32, precision=HIGHEST) + br_ref[...]
    n_e, t = logits.shape
    e_iota = lax.broadcasted_iota(jnp.int32, (n_e, t), 0)
    vals, idxs = [], []
    rest = logits
    for _ in range(TOP_K):
        m = jnp.max(rest, axis=0, keepdims=True)
        ik = jnp.min(jnp.where(rest == m, e_iota, n_e), axis=0, keepdims=True)
        vals.append(m)
        idxs.append(ik)
        rest = jnp.where(e_iota == ik, -jnp.inf, rest)
    ex = [jnp.exp(v - vals[0]) for v in vals]
    den = ex[0] + ex[1] + ex[2] + ex[3]
    sel = (e_iota == idxs[0]) | (e_iota == idxs[1]) | (e_iota == idxs[2]) | (e_iota == idxs[3])

    before = (lax.broadcasted_iota(jnp.int32, (t, t), 0)
              < lax.broadcasted_iota(jnp.int32, (t, t), 1)).astype(BF16)
    prefix = jnp.dot(sel.astype(BF16), before, preferred_element_type=F32)
    pos = carry_ref[...] + prefix
    for k in range(TOP_K):
        idx_ref[0, k:k + 1, :] = idxs[k]
        tw_ref[0, k:k + 1, :] = ex[k] / den
        rk = jnp.sum(jnp.where(e_iota == idxs[k], pos, 0.0), axis=0, keepdims=True)
        rank_ref[0, k:k + 1, :] = rk.astype(jnp.int32)
    total = carry_ref[...] + jnp.sum(sel.astype(F32), axis=1, keepdims=True)
    carry_ref[...] = total
    cnt_ref[...] = total.astype(jnp.int32)


def _out_projection(o_mla, o_diff, x, b0, mod, w_out, g_norm2, w_router, b_router):
    _, s, d = x.shape
    b = o_mla.shape[0]
    tm = min(TOKEN_TILE, s)
    nt = s // tm
    half = o_mla.shape[-1]
    full = lambda a: pl.BlockSpec(a.shape, lambda i, j: (0,) * a.ndim)
    tile = lambda w: pl.BlockSpec((1, tm, w), lambda i, j: (i, j, 0))
    x_tile = pl.BlockSpec((1, tm, d), lambda i, j: (b0 + i, j, 0))
    route = pl.BlockSpec((1, TOP_K, tm), lambda i, j: (i * nt + j, 0, 0))
    g2 = g_norm2.reshape(1, d)
    wr = w_router.T
    br = b_router.reshape(N_EXPERTS, 1)
    w_out_b = w_out.astype(BF16)
    n_strips = d // (2 * STRIP)
    return pl.pallas_call(
        _outproj_kernel,
        grid=(b, nt),
        in_specs=[tile(half), tile(half), x_tile, pl.BlockSpec((1, N_MOD, d), lambda i, j: (i, 0, 0)),
                  full(w_out_b), full(g2), full(wr), full(br)],
        out_specs=[tile(d), pl.BlockSpec((n_strips, 1, tm, STRIP), lambda i, j: (0, i, j, 0)),
                   route, route, route,
                   pl.BlockSpec((N_EXPERTS, 1), lambda i, j: (0, 0))],
        out_shape=[jax.ShapeDtypeStruct((b, s, d), F32),
                   jax.ShapeDtypeStruct((n_strips, b, s, STRIP), jnp.uint32),
                   jax.ShapeDtypeStruct((b * nt, TOP_K, tm), jnp.int32),
                   jax.ShapeDtypeStruct((b * nt, TOP_K, tm), F32),
                   jax.ShapeDtypeStruct((b * nt, TOP_K, tm), jnp.int32),
                   jax.ShapeDtypeStruct((N_EXPERTS, 1), jnp.int32)],
        scratch_shapes=[pltpu.VMEM((N_EXPERTS, 1), F32)],
        compiler_params=_params(("arbitrary", "arbitrary")),
        name="out_projection_router",
    )(o_mla, o_diff, x, mod, w_out_b, g2, wr, br)


def _split_gate_up_perm():
    r = lax.broadcasted_iota(jnp.int32, (MXU_COLS, MXU_COLS), 0)
    c = lax.broadcasted_iota(jnp.int32, (MXU_COLS, MXU_COLS), 1)
    src = jnp.where(c < LANES, 2 * c, 2 * (c - LANES) + 1)
    return (r == src).astype(BF16)


def _expert_kernel(ie_ref, ib_ref, lo_ref, hi_ref, nv_ref, xs_ref, wgu_ref, bgu_ref, wd_ref, bd_ref,
                   *rest):
    y_ref, wgu_s, wd_s, y_s = rest[-4:]
    i = pl.program_id(0)
    prev = jnp.maximum(i - 1, 0)
    valid = i < nv_ref[0]
    new_expert = (i == 0) | (ie_ref[i] != ie_ref[prev])
    new_block = (i == 0) | (ib_ref[i] != ib_ref[prev])
    n_groups = wgu_s.shape[1] // MXU_COLS
    n_strips = y_ref.shape[0]
    row = lax.broadcasted_iota(jnp.int32, (y_ref.shape[1], 1), 0)
    mine = (row >= lo_ref[i]) & (row < hi_ref[i])

    @pl.when(valid & new_expert)
    def _():
        perm = _split_gate_up_perm()
        for c in range(n_groups):
            cols = slice(c * MXU_COLS, (c + 1) * MXU_COLS)
            w = wgu_ref[0, :, cols].astype(BF16)
            wgu_s[:, cols] = jnp.dot(w, perm, preferred_element_type=F32).astype(BF16)
        wd_s[...] = wd_ref[0].astype(BF16)

    @pl.when(valid)
    def _():
        lo, hi = _unpack_pairs(jnp.concatenate([xs_ref[c] for c in range(n_strips)], axis=1))
        xb = jnp.concatenate([lo, hi], axis=1).astype(BF16)
        gu = jnp.dot(xb, wgu_s[...], preferred_element_type=F32) + bgu_ref[0]
        acts = []
        for c in range(n_groups):
            g = jnp.minimum(gu[:, c * MXU_COLS:c * MXU_COLS + LANES], SWIGLU_LIMIT)
            u = jnp.clip(gu[:, c * MXU_COLS + LANES:(c + 1) * MXU_COLS], -SWIGLU_LIMIT, SWIGLU_LIMIT)
            acts.append((g / (1.0 + jnp.exp(-SWIGLU_ALPHA * g)) * (u + 1.0)).astype(BF16))
        act = jnp.concatenate(acts, axis=1)
        y_s[...] = _pack_pairs(jnp.dot(act, wd_s[...], preferred_element_type=F32) + bd_ref[0])

    @pl.when(valid & new_block)
    def _():
        for c in range(n_strips):
            y_ref[c] = jnp.where(mine, y_s[:, c * STRIP:(c + 1) * STRIP], jnp.uint32(0))

    @pl.when(valid & jnp.logical_not(new_block))
    def _():
        for c in range(n_strips):
            y_ref[c] = jnp.where(mine, y_s[:, c * STRIP:(c + 1) * STRIP], y_ref[c])


def _expert_ffn(items, xs, blk0, n_rows, y_so_far, w_gate_up, b_gate_up_grouped, w_down, b_down):
    n_strips = xs.shape[0]
    d = 2 * n_strips * STRIP
    dff2 = w_gate_up.shape[-1]
    dff = dff2 // 2
    n_items = items[0].shape[0]
    wmap = lambda i, ie, ib, lo, hi, nv: (ie[i], 0, 0)
    in_specs = [pl.BlockSpec((n_strips, MOE_ROWS, STRIP), lambda i, ie, ib, lo, hi, nv: (0, ib[i], 0)),
                pl.BlockSpec((1, d, dff2), wmap), pl.BlockSpec((1, 1, dff2), wmap),
                pl.BlockSpec((1, dff, d), wmap), pl.BlockSpec((1, 1, d), wmap)]
    operands = [*items, xs, w_gate_up, b_gate_up_grouped, w_down, b_down]
    aliases = {}
    if y_so_far is not None:
        in_specs.append(pl.BlockSpec(memory_space=pl.ANY))
        operands.append(y_so_far)
        aliases = {len(operands) - 1: 0}
    grid_spec = pltpu.PrefetchScalarGridSpec(
        num_scalar_prefetch=5,
        grid=(n_items,),
        in_specs=in_specs,
        out_specs=pl.BlockSpec((n_strips, MOE_ROWS, STRIP),
                               lambda i, ie, ib, lo, hi, nv: (0, blk0 + ib[i], 0)),
        scratch_shapes=[pltpu.VMEM((d, dff2), BF16), pltpu.VMEM((dff, d), BF16),
                        pltpu.VMEM((MOE_ROWS, d // 2), jnp.uint32)],
    )
    return pl.pallas_call(
        _expert_kernel,
        grid_spec=grid_spec,
        out_shape=jax.ShapeDtypeStruct((n_strips, n_rows, STRIP), jnp.uint32),
        input_output_aliases=aliases,
        compiler_params=_params(("arbitrary",), EXPERT_VMEM_LIMIT),
        name="expert_ffn",
    )(*operands)


def _items_for_blocks(items, first_blk, n_blk):
    expert, block, lo, hi, total = items
    valid = jnp.arange(block.shape[0], dtype=jnp.int32) < total
    before = jnp.sum(valid & (block < first_blk)).astype(jnp.int32)
    inside = jnp.sum(valid & (block >= first_blk) & (block < first_blk + n_blk)).astype(jnp.int32)
    steps = jnp.arange(n_blk + N_EXPERTS - 1, dtype=jnp.int32)
    pos = before + jnp.clip(steps, 0, jnp.maximum(inside - 1, 0))
    return expert[pos], block[pos] - first_blk, lo[pos], hi[pos], inside.reshape(1)


def _expert_work_items(counts, n_rows):
    e_ids = jnp.arange(N_EXPERTS, dtype=jnp.int32)
    ends = jnp.cumsum(counts).astype(jnp.int32)
    starts = ends - counts
    first_blk = starts // MOE_ROWS
    last_blk = (ends - 1) // MOE_ROWS
    per_expert = jnp.where(counts > 0, last_blk - first_blk + 1, 0)
    item_end = jnp.cumsum(per_expert).astype(jnp.int32)
    item_start = item_end - per_expert
    total = item_end[-1:]
    max_items = n_rows // MOE_ROWS + N_EXPERTS - 1
    it = jnp.clip(jnp.arange(max_items, dtype=jnp.int32), 0, jnp.maximum(total - 1, 0))
    expert = jnp.sum(item_end[None, :] <= it[:, None], axis=1).astype(jnp.int32)
    pick = lambda v: jnp.sum(jnp.where(expert[:, None] == e_ids, v, 0), axis=1)
    block = pick(first_blk) + it - pick(item_start)
    lo = jnp.maximum(pick(starts) - block * MOE_ROWS, 0)
    hi = jnp.minimum(pick(ends) - block * MOE_ROWS, MOE_ROWS)
    return expert, block, lo, hi, total, starts


def _gather_strips(strips, idx):
    c, r, w = strips.shape
    offsets = jnp.arange(c, dtype=jnp.int32)[:, None] * r
    out = _gather_rows(strips.reshape(c * r, w), (idx[None, :] + offsets).reshape(-1))
    return out.reshape(c, idx.shape[0], w)


def _gather_rows(table, idx):
    m = idx.shape[0]
    w = table.shape[1]
    window = GATHER_WINDOW
    mesh = plsc.VectorSubcoreMesh(core_axis_name="core", subcore_axis_name="subcore")

    @pl.kernel(out_type=jax.ShapeDtypeStruct((m, w), table.dtype), mesh=mesh)
    def gather_kernel(table_hbm, idx_hbm, out_hbm):
        def body(idx_vmem, out_vmem):
            pltpu.sync_copy(table_hbm.at[idx_vmem.at[0]], out_vmem)

        pltpu.emit_pipeline(
            body,
            grid=(m // window,),
            in_specs=[pl.BlockSpec((1, window), index_map=lambda i: (0, i))],
            out_specs=[pl.BlockSpec((window, w), index_map=lambda i: (i, 0))],
            core_axis_name=("core", "subcore"),
            dimension_semantics=(pltpu.PARALLEL,),
        )(idx_hbm, out_hbm)

    return gather_kernel(table, idx.reshape(1, m))


def _combine_kernel(yk_ref, tw_ref, x1_ref, mod_ref, gf_ref, *rest):
    o_ref = rest[-1]
    tw = tw_ref[0]
    lows, highs = [], []
    for c in range(yk_ref.shape[0]):
        lo, hi = _unpack_pairs(yk_ref[c, 0, 0])
        lo, hi = lo * tw[:, 0:1], hi * tw[:, 0:1]
        for k in range(1, TOP_K):
            lo_k, hi_k = _unpack_pairs(yk_ref[c, k, 0])
            lo, hi = lo + lo_k * tw[:, k:k + 1], hi + hi_k * tw[:, k:k + 1]
        lows.append(lo)
        highs.append(hi)
    y = jnp.concatenate(lows + highs, axis=1)
    x2 = x1_ref[0] + mod_ref[0][5:6] * y
    o_ref[0] = _rms(x2, gf_ref[...], 1e-6)


def _combine(yk, tw, x1, mod, g_final, b0, b_total, out_so_far):
    b, s, d = x1.shape
    tm = min(TOKEN_TILE, s)
    gf = g_final.reshape(1, d)
    in_specs = [pl.BlockSpec((d // (2 * STRIP), TOP_K, 1, tm, STRIP), lambda i, j: (0, 0, i, j, 0)),
                pl.BlockSpec((1, tm, TOP_K), lambda i, j: (i, j, 0)),
                pl.BlockSpec((1, tm, d), lambda i, j: (i, j, 0)),
                pl.BlockSpec((1, N_MOD, d), lambda i, j: (i, 0, 0)),
                pl.BlockSpec((1, d), lambda i, j: (0, 0))]
    operands = [yk, tw, x1, mod, gf]
    aliases = {}
    if out_so_far is not None:
        in_specs.append(pl.BlockSpec(memory_space=pl.ANY))
        operands.append(out_so_far)
        aliases = {len(operands) - 1: 0}
    return pl.pallas_call(
        _combine_kernel,
        grid=(b, s // tm),
        in_specs=in_specs,
        out_specs=pl.BlockSpec((1, tm, d), lambda i, j: (b0 + i, j, 0)),
        out_shape=jax.ShapeDtypeStruct((b_total, s, d), F32),
        input_output_aliases=aliases,
        compiler_params=_params(("parallel", "parallel")),
        name="combine_final_norm",
    )(*operands)


def kernel(x, c, positions, w_ada, b_ada, g_norm1, w_in, g_q_norm, w_uq, g_kv_norm, w_ukv,
           lambda_q1, lambda_k1, lambda_q2, lambda_k2, g_subln, w_out, g_norm2,
           w_router, b_router, w_gate_up, b_gate_up, w_down, b_down, g_final):
    b_total, s, d = x.shape
    l = 0
    mod_all = _ada_mod(c, w_ada[l], b_ada[l]).reshape(b_total, N_MOD, d)
    w_in_p, w_dvt, w_uq_p, w_uk_p, w_uvt = _pad_in_weights(w_in[l], w_uq[l], w_ukv[l])
    lam_vecs = jnp.stack([lambda_q1[l], lambda_k1[l], lambda_q2[l], lambda_k2[l]]).astype(F32)
    bgu = (b_gate_up[l].reshape(N_EXPERTS, EXPERT_DFF // LANES, LANES, 2)
           .transpose(0, 1, 3, 2).reshape(N_EXPERTS, 1, 2 * EXPERT_DFF))
    bd = b_down[l].reshape(N_EXPERTS, 1, d)

    groups = BATCH_GROUPS if b_total % BATCH_GROUPS == 0 else 1
    b = b_total // groups
    n = b * s
    out = None
    for g in range(groups):
        b0 = g * b
        mod = mod_all[b0:b0 + b]
        qm, km, vmt, dq, dk, dvt = _in_projection(x, b0, positions[b0:b0 + b], mod, g_norm1[l],
                                                  w_in_p, w_dvt, g_q_norm[l], w_uq_p,
                                                  g_kv_norm[l], w_uk_p, w_uvt)
        o_mla = _mla_attention(qm, km, vmt)
        o_diff = _diff_attention(lam_vecs, g_subln[l], dq, dk, dvt)
        x1, h2, idx, tw, rank, cnt = _out_projection(o_mla, o_diff, x, b0, mod, w_out[l],
                                                     g_norm2[l], w_router[l], b_router[l])

        to_kn = lambda a: a.transpose(1, 0, 2).reshape(TOP_K, n)
        idx, tw, rank = to_kn(idx), to_kn(tw), to_kn(rank)
        *items, starts = _expert_work_items(cnt[:, 0], n * TOP_K)
        e_ids = jnp.arange(N_EXPERTS, dtype=jnp.int32)
        slot = rank + jnp.sum(jnp.where(idx[..., None] == e_ids, starts, 0), axis=-1)
        tok = jnp.arange(n, dtype=jnp.int32)
        row_tok = jnp.sort((idx * n + tok).reshape(-1)) % n

        n_rows = n * TOP_K
        n_blk = n_rows // MOE_ROWS
        row_groups = ROW_GROUPS if n_blk % ROW_GROUPS == 0 else 1
        g_blk = n_blk // row_groups
        h2s = h2.reshape(-1, n, STRIP)
        y = None
        for r in range(row_groups):
            rows = slice(r * g_blk * MOE_ROWS, (r + 1) * g_blk * MOE_ROWS)
            xs = _gather_strips(h2s, row_tok[rows])
            y = _expert_ffn(_items_for_blocks(items, r * g_blk, g_blk), xs, r * g_blk, n_rows, y,
                            w_gate_up[l], bgu, w_down[l], bd)
        yk = _gather_strips(y, slot.reshape(-1)).reshape(-1, TOP_K, b, s, STRIP)
        out = _combine(yk, tw.T.reshape(b, s, TOP_K), x1, mod, g_final, b0, b_total, out)
    return out
```

```python
import functools
import math

import jax
import jax.numpy as jnp
from jax import lax
from jax.experimental import pallas as pl
from jax.experimental.pallas import tpu as pltpu
from jax.experimental.pallas import tpu_sc as plsc

F32 = jnp.float32
BF16 = jnp.bfloat16
HIGHEST = lax.Precision.HIGHEST

D_MODEL = 1024
CHUNK = 64
ROPE_THETA = 500000.0
MLA_HEADS = 8
MLA_NOPE = 64
MLA_ROPE = 32
MLA_V = 64
MLA_QK = MLA_NOPE + MLA_ROPE
Q_RANK = 384
KV_RANK = 256
DIFF_HEADS = 4
DIFF_DIM = 64
DIFF_ROPE = DIFF_DIM // 4
DIFF_WIDTH = DIFF_HEADS * 2 * DIFF_DIM
N_EXPERTS = 32
TOP_K = 4
EXPERT_DFF = D_MODEL
SWIGLU_LIMIT = 7.0
SWIGLU_ALPHA = 1.702
N_MOD = 6
LAMBDA_INIT = 0.8 - 0.6 * math.exp(-0.3 * 0)
LOG2E = math.log2(math.e)

LANES = 128
MXU_COLS = 256
HEAD_PAD = LANES
TOKEN_TILE = 512
MLA_TILE = 512
DIFF_TILE = 256
BATCH_GROUPS = 1
ROW_GROUPS = 4
TOKEN_GROUPS = 4
MOE_ROWS = 512
GATHER_WINDOW = 128
STRIP = 256
VMEM_LIMIT = 48 * 1024 * 1024
EXPERT_VMEM_LIMIT = 58 * 1024 * 1024
NEG_BIG = -1e30

MLA_X1 = 96
MLA_HALF = MLA_ROPE // 2
DIFF_HALF = DIFF_ROPE // 2

C_Q = 0
C_KV = Q_RANK
C_KPE = Q_RANK + KV_RANK
C_DQ = C_KPE + LANES
C_DK = C_DQ + DIFF_WIDTH
IN_COLS_PAD = C_DK + DIFF_WIDTH


def _rms(x, g, eps):
    return x * lax.rsqrt(jnp.mean(x * x, axis=-1, keepdims=True) + eps) * g


def _pack_pairs(x):
    w = x.shape[1] // 2
    as_bits = lambda v: lax.bitcast_convert_type(v.astype(BF16).astype(F32), jnp.uint32)
    return (as_bits(x[:, w:]) & jnp.uint32(0xFFFF0000)) | (as_bits(x[:, :w]) >> 16)


def _unpack_pairs(p):
    lo = lax.bitcast_convert_type(p << 16, F32)
    hi = lax.bitcast_convert_type(p & jnp.uint32(0xFFFF0000), F32)
    return lo, hi


def _params(sem, limit=VMEM_LIMIT):
    return pltpu.CompilerParams(dimension_semantics=sem, vmem_limit_bytes=limit)


def _nt(a, b):
    return lax.dot_general(a, b, (((1,), (1,)), ((), ())), preferred_element_type=F32)


def _ada_kernel(c_ref, w_ref, b_ref, o_ref):
    c = c_ref[...]
    sc = c / (1.0 + jnp.exp(-c))
    o_ref[...] = jnp.dot(sc, w_ref[...], preferred_element_type=F32, precision=HIGHEST) + b_ref[...]


def _ada_mod(c, w_ada, b_ada):
    b, d = c.shape
    n = w_ada.shape[1]
    return pl.pallas_call(
        _ada_kernel,
        grid=(n // d,),
        in_specs=[pl.BlockSpec((b, d), lambda j: (0, 0)),
                  pl.BlockSpec((d, d), lambda j: (0, j)),
                  pl.BlockSpec((1, d), lambda j: (0, j))],
        out_specs=pl.BlockSpec((b, d), lambda j: (0, j)),
        out_shape=jax.ShapeDtypeStruct((b, n), F32),
        compiler_params=_params(("arbitrary",)),
        name="ada_mod",
    )(c, w_ada, b_ada.reshape(1, n))


def _rope(t, cos, sin_a, sin_b, shift):
    return (t * cos + pltpu.roll(t, LANES - shift, 1) * sin_a + pltpu.roll(t, shift, 1) * sin_b)


def _inproj_kernel(x_ref, pos_ref, mod_ref, g1_ref, win_ref, wdvt_ref, gq_ref, wuq_ref, gkv_ref,
                   wuk_ref, wuvt_ref, freq_ref, mask_ref,
                   qm_ref, km_ref, vmt_ref, dq_ref, dk_ref, dvt_ref):
    x = x_ref[0]
    mod = mod_ref[0]
    h = (_rms(x, g1_ref[...], 1e-6) * (1.0 + mod[1:2]) + mod[0:1]).astype(BF16)
    proj = jnp.dot(h, win_ref[...], preferred_element_type=F32)
    dvt_ref[0] = _nt(wdvt_ref[...], h).astype(BF16)

    ang = pos_ref[0].astype(F32) * freq_ref[...]
    cos, sin = jnp.cos(ang), jnp.sin(ang)
    mk = mask_ref[...]
    cos_m = jnp.where(mk[0:1] > 0, cos, 1.0)
    sa_m, sb_m = sin * mk[1:2], sin * mk[2:3]
    cos_d = jnp.where(mk[3:4] > 0, cos, 1.0)
    sa_d, sb_d = sin * mk[4:5], sin * mk[5:6]
    mla_scale = MLA_QK ** -0.5 * LOG2E
    diff_scale = DIFF_DIM ** -0.5 * LOG2E

    cq = _rms(proj[:, C_Q:C_Q + Q_RANK], gq_ref[...], 1e-6)
    q = jnp.dot(cq.astype(BF16), wuq_ref[...], preferred_element_type=F32)
    ckv = _rms(proj[:, C_KV:C_KV + KV_RANK], gkv_ref[...], 1e-6).astype(BF16)
    kn = jnp.dot(ckv, wuk_ref[...], preferred_element_type=F32)
    vmt_ref[0] = _nt(wuvt_ref[...], ckv).astype(BF16)
    kpe = _rope(proj[:, C_KPE:C_KPE + LANES], cos_m, sa_m, sb_m, MLA_HALF)
    cq_m, saq_m, sbq_m = cos_m * mla_scale, sa_m * mla_scale, sb_m * mla_scale
    for hd in range(MLA_HEADS):
        sl = slice(hd * HEAD_PAD, (hd + 1) * HEAD_PAD)
        qm_ref[0, :, sl] = _rope(q[:, sl], cq_m, saq_m, sbq_m, MLA_HALF).astype(BF16)
        km_ref[0, :, sl] = (kn[:, sl] + kpe).astype(BF16)

    cq_d, saq_d, sbq_d = cos_d * diff_scale, sa_d * diff_scale, sb_d * diff_scale
    for hd in range(DIFF_HEADS):
        sl = slice(hd * LANES, (hd + 1) * LANES)
        tq = proj[:, C_DQ + hd * LANES:C_DQ + (hd + 1) * LANES]
        tk = proj[:, C_DK + hd * LANES:C_DK + (hd + 1) * LANES]
        dq_ref[0, :, sl] = _rope(tq, cq_d, saq_d, sbq_d, DIFF_HALF).astype(BF16)
        dk_ref[0, :, sl] = _rope(tk, cos_d, sa_d, sb_d, DIFF_HALF).astype(BF16)


def _rope_tables():
    lane = jnp.arange(LANES)
    f_m = ROPE_THETA ** (-jnp.arange(MLA_HALF, dtype=F32) / MLA_HALF)
    f_d = ROPE_THETA ** (-jnp.arange(DIFF_HALF, dtype=F32) / DIFF_HALF)
    m_x1 = (lane >= MLA_X1) & (lane < MLA_X1 + MLA_HALF)
    m_x2 = (lane >= MLA_X1 + MLA_HALF) & (lane < MLA_X1 + 2 * MLA_HALF)
    d_x1 = (lane % DIFF_DIM) < DIFF_HALF
    d_x2 = ((lane % DIFF_DIM) >= DIFF_HALF) & ((lane % DIFF_DIM) < 2 * DIFF_HALF)
    freq = jnp.where(m_x1 | m_x2, f_m[(lane - MLA_X1) % MLA_HALF], 0.0)
    freq = jnp.where(d_x1 | d_x2, f_d[lane % DIFF_HALF], freq)
    z = jnp.zeros((LANES,), F32)
    masks = jnp.stack([(m_x1 | m_x2).astype(F32), -m_x1.astype(F32), m_x2.astype(F32),
                       (d_x1 | d_x2).astype(F32), -d_x1.astype(F32), d_x2.astype(F32), z, z])
    return freq.reshape(1, LANES).astype(F32), masks


def _pad_in_weights(w_in, w_uq, w_ukv):
    d = w_in.shape[0]
    c_kpe = Q_RANK + KV_RANK
    c_dv = c_kpe + MLA_ROPE + 2 * DIFF_WIDTH
    kpe = jnp.concatenate([jnp.zeros((d, MLA_X1), F32), w_in[:, c_kpe:c_kpe + MLA_ROPE]], axis=1)
    w_in_p = jnp.concatenate([w_in[:, :c_kpe], kpe, w_in[:, c_kpe + MLA_ROPE:c_dv]], axis=1)
    w_dvt = w_in[:, c_dv:].T
    uq = w_uq.reshape(Q_RANK, MLA_HEADS, MLA_QK)
    uq_p = jnp.concatenate([uq[:, :, :MLA_NOPE],
                            jnp.zeros((Q_RANK, MLA_HEADS, MLA_X1 - MLA_NOPE), F32),
                            uq[:, :, MLA_NOPE:]], axis=2).reshape(Q_RANK, MLA_HEADS * HEAD_PAD)
    ukv = w_ukv.reshape(KV_RANK, MLA_HEADS, MLA_NOPE + MLA_V)
    uk_p = jnp.concatenate([ukv[:, :, :MLA_NOPE],
                            jnp.zeros((KV_RANK, MLA_HEADS, HEAD_PAD - MLA_NOPE), F32)],
                           axis=2).reshape(KV_RANK, MLA_HEADS * HEAD_PAD)
    uvt = ukv[:, :, MLA_NOPE:].reshape(KV_RANK, MLA_HEADS * MLA_V).T
    return (w_in_p.astype(BF16), w_dvt.astype(BF16), uq_p.astype(BF16), uk_p.astype(BF16),
            uvt.astype(BF16))


def _in_projection(x, b0, positions, mod, g_norm1, w_in_p, w_dvt, g_q, w_uq_p, g_kv, w_uk_p, w_uvt):
    _, s, d = x.shape
    b = positions.shape[0]
    tm = min(TOKEN_TILE, s)
    freq, masks = _rope_tables()
    full = lambda a: pl.BlockSpec(a.shape, lambda i, j: (0,) * a.ndim)
    tile = lambda w: pl.BlockSpec((1, tm, w), lambda i, j: (i, j, 0))
    x_tile = pl.BlockSpec((1, tm, d), lambda i, j: (b0 + i, j, 0))
    tile_t = lambda w: pl.BlockSpec((1, w, tm), lambda i, j: (i, 0, j))
    wide, vm_w = MLA_HEADS * HEAD_PAD, MLA_HEADS * MLA_V
    g1, gq, gkv = g_norm1.reshape(1, d), g_q.reshape(1, Q_RANK), g_kv.reshape(1, KV_RANK)
    tok = lambda w: jax.ShapeDtypeStruct((b, s, w), BF16)
    feat = lambda w: jax.ShapeDtypeStruct((b, w, s), BF16)
    return pl.pallas_call(
        _inproj_kernel,
        grid=(b, s // tm),
        in_specs=[x_tile, tile(1), pl.BlockSpec((1, N_MOD, d), lambda i, j: (i, 0, 0)),
                  full(g1), full(w_in_p), full(w_dvt), full(gq), full(w_uq_p), full(gkv),
                  full(w_uk_p), full(w_uvt), full(freq), full(masks)],
        out_specs=[tile(wide), tile(wide), tile_t(vm_w), tile(DIFF_WIDTH), tile(DIFF_WIDTH),
                   tile_t(DIFF_WIDTH)],
        out_shape=[tok(wide), tok(wide), feat(vm_w), tok(DIFF_WIDTH), tok(DIFF_WIDTH),
                   feat(DIFF_WIDTH)],
        compiler_params=_params(("parallel", "parallel")),
        name="in_projection",
    )(x, positions.reshape(b, s, 1), mod, g1, w_in_p, w_dvt, gq, w_uq_p, gkv, w_uk_p, w_uvt,
      freq, masks)


def _diag_mask(t):
    key = lax.broadcasted_iota(jnp.int32, (t, t), 0) // CHUNK
    qry = lax.broadcasted_iota(jnp.int32, (t, t), 1) // CHUNK
    return key <= qry


def _scores_t(q, k_ref, lanes, q0, t):
    diag = jnp.where(_diag_mask(t), _nt(k_ref[0, q0:q0 + t, lanes], q), NEG_BIG)
    bulk = _nt(k_ref[0, 0:q0, lanes], q) if q0 else None
    return bulk, diag


def _softmax_t(bulk, diag):
    m = jnp.max(diag, axis=0, keepdims=True)
    if bulk is not None:
        m = jnp.maximum(m, jnp.max(bulk, axis=0, keepdims=True))
    p_diag = jnp.exp2(diag - m)
    l = jnp.sum(p_diag, axis=0, keepdims=True)
    p_bulk = None
    if bulk is not None:
        p_bulk = jnp.exp2(bulk - m)
        l = l + jnp.sum(p_bulk, axis=0, keepdims=True)
    return p_bulk, p_diag, l


def _pv_t(vt_ref, rows, p_bulk, p_diag, q0, t):
    o = jnp.dot(vt_ref[0, rows, q0:q0 + t], p_diag.astype(BF16), preferred_element_type=F32)
    if p_bulk is not None:
        o = o + jnp.dot(vt_ref[0, rows, 0:q0], p_bulk.astype(BF16), preferred_element_type=F32)
    return o


def _mla_attn_kernel(q_ref, k_ref, vt_ref, o_ref, *, seq, t):
    units = [(q0, i) for q0 in range(0, seq, t) for i in range(2)]

    def scores(q0, i):
        lanes = slice(i * HEAD_PAD, (i + 1) * HEAD_PAD)
        return _scores_t(q_ref[0, q0:q0 + t, lanes], k_ref, lanes, q0, t)

    ahead = scores(*units[0])
    outs = []
    for u, (q0, i) in enumerate(units):
        bulk, diag = ahead
        ahead = scores(*units[u + 1]) if u + 1 < len(units) else None
        p_bulk, p_diag, l = _softmax_t(bulk, diag)
        rows = slice(i * MLA_V, (i + 1) * MLA_V)
        outs.append(_pv_t(vt_ref, rows, p_bulk, p_diag, q0, t) / l)
        if i == 1:
            o_ref[0, q0:q0 + t, :] = jnp.concatenate(outs, axis=0).T.astype(BF16)
            outs = []


def _diff_attn_kernel(lam_ref, gs_ref, q_ref, k_ref, vt_ref, o_ref, *, seq, t):
    lv = lam_ref[...]
    lam = (jnp.exp(jnp.sum(lv[0:1] * lv[1:2], axis=-1, keepdims=True))
           - jnp.exp(jnp.sum(lv[2:3] * lv[3:4], axis=-1, keepdims=True)) + LAMBDA_INIT)
    lane = lax.broadcasted_iota(jnp.int32, (t, 2 * DIFF_DIM), 1)
    every = slice(0, 2 * DIFF_DIM)
    def scores(q0):
        q = q_ref[0, q0:q0 + t, :]
        zero = jnp.zeros_like(q)
        return (_scores_t(jnp.where(lane < DIFF_DIM, q, zero), k_ref, every, q0, t),
                _scores_t(jnp.where(lane >= DIFF_DIM, q, zero), k_ref, every, q0, t))

    ahead = scores(0)
    for q0 in range(0, seq, t):
        (b1, d1), (b2, d2) = ahead
        ahead = scores(q0 + t) if q0 + t < seq else None
        pb1, pd1, l1 = _softmax_t(b1, d1)
        pb2, pd2, l2 = _softmax_t(b2, d2)
        c1, c2 = 1.0 / l1, lam / l2
        w_diag = pd1 * c1 - pd2 * c2
        w_bulk = (pb1 * c1 - pb2 * c2) if q0 else None
        o = _pv_t(vt_ref, every, w_bulk, w_diag, q0, t)
        o = (o * lax.rsqrt(jnp.mean(o * o, axis=0, keepdims=True) + 1e-5) * gs_ref[...]
             * (1.0 - LAMBDA_INIT))
        o_ref[0, q0:q0 + t, :] = o.T.astype(BF16)


def _mla_attention(qm, km, vmt):
    b, s, _ = qm.shape
    t = min(MLA_TILE, s)
    pair = 2 * HEAD_PAD
    return pl.pallas_call(
        functools.partial(_mla_attn_kernel, seq=s, t=t),
        grid=(b, MLA_HEADS // 2),
        in_specs=[pl.BlockSpec((1, s, pair), lambda i, j: (i, 0, j)),
                  pl.BlockSpec((1, s, pair), lambda i, j: (i, 0, j)),
                  pl.BlockSpec((1, LANES, s), lambda i, j: (i, j, 0))],
        out_specs=pl.BlockSpec((1, s, LANES), lambda i, j: (i, 0, j)),
        out_shape=jax.ShapeDtypeStruct((b, s, MLA_HEADS * MLA_V), BF16),
        compiler_params=_params(("parallel", "parallel")),
        name="mla_attention",
    )(qm, km, vmt)


def _diff_attention(lam_vecs, g_subln, dq, dk, dvt):
    b, s, _ = dq.shape
    t = min(DIFF_TILE, s)
    blk = pl.BlockSpec((1, s, LANES), lambda i, j: (i, 0, j))
    gs = g_subln.reshape(2 * DIFF_DIM, 1)
    return pl.pallas_call(
        functools.partial(_diff_attn_kernel, seq=s, t=t),
        grid=(b, DIFF_HEADS),
        in_specs=[pl.BlockSpec(lam_vecs.shape, lambda i, j: (0, 0)),
                  pl.BlockSpec(gs.shape, lambda i, j: (0, 0)), blk, blk,
                  pl.BlockSpec((1, LANES, s), lambda i, j: (i, j, 0))],
        out_specs=blk,
        out_shape=jax.ShapeDtypeStruct((b, s, DIFF_WIDTH), BF16),
        compiler_params=_params(("parallel", "parallel")),
        name="diff_attention",
    )(lam_vecs, gs, dq, dk, dvt)


def _outproj_kernel(om_ref, od_ref, x_ref, mod_ref, wout_ref, g2_ref, wr_ref, br_ref,
                    x1_ref, h2_ref, idx_ref, tw_ref, rank_ref, cnt_ref, carry_ref):
    first = (pl.program_id(0) == 0) & (pl.program_id(1) == 0)

    @pl.when(first)
    def _():
        carry_ref[...] = jnp.zeros(carry_ref.shape, F32)

    half = om_ref.shape[-1]
    mix = (jnp.dot(om_ref[0], wout_ref[:half, :], preferred_element_type=F32)
           + jnp.dot(od_ref[0], wout_ref[half:, :], preferred_element_type=F32))
    mod = mod_ref[0]
    x1 = x_ref[0] + mod[2:3] * mix
    h2 = _rms(x1, g2_ref[...], 1e-6) * (1.0 + mod[4:5]) + mod[3:4]
    x1_ref[0] = x1
    packed = _pack_pairs(h2)
    for c in range(h2_ref.shape[0]):
        h2_ref[c, 0] = packed[:, c * STRIP:(c + 1) * STRIP]

    logits = lax.dot_general(wr_ref[...], h2, (((1,), (1,)), ((), ())),
                             preferred_element_type=F32, precision=HIGHEST) + br_ref[...]
    n_e, t = logits.shape
    e_iota = lax.broadcasted_iota(jnp.int32, (n_e, t), 0)
    vals, idxs = [], []
    rest = logits
    for _ in range(TOP_K):
        m = jnp.max(rest, axis=0, keepdims=True)
        ik = jnp.min(jnp.where(rest == m, e_iota, n_e), axis=0, keepdims=True)
        vals.append(m)
        idxs.append(ik)
        rest = jnp.where(e_iota == ik, -jnp.inf, rest)
    ex = [jnp.exp(v - vals[0]) for v in vals]
    den = ex[0] + ex[1] + ex[2] + ex[3]
    sel = (e_iota == idxs[0]) | (e_iota == idxs[1]) | (e_iota == idxs[2]) | (e_iota == idxs[3])

    before = (lax.broadcasted_iota(jnp.int32, (t, t), 0)
              < lax.broadcasted_iota(jnp.int32, (t, t), 1)).astype(BF16)
    prefix = jnp.dot(sel.astype(BF16), before, preferred_element_type=F32)
    pos = carry_ref[...] + prefix
    for k in range(TOP_K):
        idx_ref[0, k:k + 1, :] = idxs[k]
        tw_ref[0, k:k + 1, :] = ex[k] / den
        rk = jnp.sum(jnp.where(e_iota == idxs[k], pos, 0.0), axis=0, keepdims=True)
        rank_ref[0, k:k + 1, :] = rk.astype(jnp.int32)
    total = carry_ref[...] + jnp.sum(sel.astype(F32), axis=1, keepdims=True)
    carry_ref[...] = total
    cnt_ref[...] = total.astype(jnp.int32)


def _out_projection(o_mla, o_diff, x, b0, mod, w_out, g_norm2, w_router, b_router):
    _, s, d = x.shape
    b = o_mla.shape[0]
    tm = min(TOKEN_TILE, s)
    nt = s // tm
    half = o_mla.shape[-1]
    full = lambda a: pl.BlockSpec(a.shape, lambda i, j: (0,) * a.ndim)
    tile = lambda w: pl.BlockSpec((1, tm, w), lambda i, j: (i, j, 0))
    x_tile = pl.BlockSpec((1, tm, d), lambda i, j: (b0 + i, j, 0))
    route = pl.BlockSpec((1, TOP_K, tm), lambda i, j: (i * nt + j, 0, 0))
    g2 = g_norm2.reshape(1, d)
    wr = w_router.T
    br = b_router.reshape(N_EXPERTS, 1)
    w_out_b = w_out.astype(BF16)
    n_strips = d // (2 * STRIP)
    return pl.pallas_call(
        _outproj_kernel,
        grid=(b, nt),
        in_specs=[tile(half), tile(half), x_tile, pl.BlockSpec((1, N_MOD, d), lambda i, j: (i, 0, 0)),
                  full(w_out_b), full(g2), full(wr), full(br)],
        out_specs=[tile(d), pl.BlockSpec((n_strips, 1, tm, STRIP), lambda i, j: (0, i, j, 0)),
                   route, route, route,
                   pl.BlockSpec((N_EXPERTS, 1), lambda i, j: (0, 0))],
        out_shape=[jax.ShapeDtypeStruct((b, s, d), F32),
                   jax.ShapeDtypeStruct((n_strips, b, s, STRIP), jnp.uint32),
                   jax.ShapeDtypeStruct((b * nt, TOP_K, tm), jnp.int32),
                   jax.ShapeDtypeStruct((b * nt, TOP_K, tm), F32),
                   jax.ShapeDtypeStruct((b * nt, TOP_K, tm), jnp.int32),
                   jax.ShapeDtypeStruct((N_EXPERTS, 1), jnp.int32)],
        scratch_shapes=[pltpu.VMEM((N_EXPERTS, 1), F32)],
        compiler_params=_params(("arbitrary", "arbitrary")),
        name="out_projection_router",
    )(o_mla, o_diff, x, mod, w_out_b, g2, wr, br)


def _split_gate_up_perm():
    r = lax.broadcasted_iota(jnp.int32, (MXU_COLS, MXU_COLS), 0)
    c = lax.broadcasted_iota(jnp.int32, (MXU_COLS, MXU_COLS), 1)
    src = jnp.where(c < LANES, 2 * c, 2 * (c - LANES) + 1)
    return (r == src).astype(BF16)


def _expert_kernel(ie_ref, ib_ref, lo_ref, hi_ref, nv_ref, xs_ref, wgu_ref, bgu_ref, wd_ref, bd_ref,
                   *rest):
    y_ref, wgu_s, wd_s, y_s = rest[-4:]
    i = pl.program_id(0)
    prev = jnp.maximum(i - 1, 0)
    valid = i < nv_ref[0]
    new_expert = (i == 0) | (ie_ref[i] != ie_ref[prev])
    new_block = (i == 0) | (ib_ref[i] != ib_ref[prev])
    n_groups = wgu_s.shape[1] // MXU_COLS
    n_strips = y_ref.shape[0]
    row = lax.broadcasted_iota(jnp.int32, (y_ref.shape[1], 1), 0)
    mine = (row >= lo_ref[i]) & (row < hi_ref[i])

    @pl.when(valid & new_expert)
    def _():
        perm = _split_gate_up_perm()
        for c in range(n_groups):
            cols = slice(c * MXU_COLS, (c + 1) * MXU_COLS)
            w = wgu_ref[0, :, cols].astype(BF16)
            wgu_s[:, cols] = jnp.dot(w, perm, preferred_element_type=F32).astype(BF16)
        wd_s[...] = wd_ref[0].astype(BF16)

    @pl.when(valid)
    def _():
        lo, hi = _unpack_pairs(jnp.concatenate([xs_ref[c] for c in range(n_strips)], axis=1))
        xb = jnp.concatenate([lo, hi], axis=1).astype(BF16)
        gu = jnp.dot(xb, wgu_s[...], preferred_element_type=F32) + bgu_ref[0]
        acts = []
        for c in range(n_groups):
            g = jnp.minimum(gu[:, c * MXU_COLS:c * MXU_COLS + LANES], SWIGLU_LIMIT)
            u = jnp.clip(gu[:, c * MXU_COLS + LANES:(c + 1) * MXU_COLS], -SWIGLU_LIMIT, SWIGLU_LIMIT)
            acts.append((g / (1.0 + jnp.exp(-SWIGLU_ALPHA * g)) * (u + 1.0)).astype(BF16))
        act = jnp.concatenate(acts, axis=1)
        y_s[...] = _pack_pairs(jnp.dot(act, wd_s[...], preferred_element_type=F32) + bd_ref[0])

    @pl.when(valid & new_block)
    def _():
        for c in range(n_strips):
            y_ref[c] = jnp.where(mine, y_s[:, c * STRIP:(c + 1) * STRIP], jnp.uint32(0))

    @pl.when(valid & jnp.logical_not(new_block))
    def _():
        for c in range(n_strips):
            y_ref[c] = jnp.where(mine, y_s[:, c * STRIP:(c + 1) * STRIP], y_ref[c])


def _expert_ffn(items, xs, blk0, n_rows, y_so_far, w_gate_up, b_gate_up_grouped, w_down, b_down):
    n_strips = xs.shape[0]
    d = 2 * n_strips * STRIP
    dff2 = w_gate_up.shape[-1]
    dff = dff2 // 2
    n_items = items[0].shape[0]
    wmap = lambda i, ie, ib, lo, hi, nv: (ie[i], 0, 0)
    in_specs = [pl.BlockSpec((n_strips, MOE_ROWS, STRIP), lambda i, ie, ib, lo, hi, nv: (0, ib[i], 0)),
                pl.BlockSpec((1, d, dff2), wmap), pl.BlockSpec((1, 1, dff2), wmap),
                pl.BlockSpec((1, dff, d), wmap), pl.BlockSpec((1, 1, d), wmap)]
    operands = [*items, xs, w_gate_up, b_gate_up_grouped, w_down, b_down]
    aliases = {}
    if y_so_far is not None:
        in_specs.append(pl.BlockSpec(memory_space=pl.ANY))
        operands.append(y_so_far)
        aliases = {len(operands) - 1: 0}
    grid_spec = pltpu.PrefetchScalarGridSpec(
        num_scalar_prefetch=5,
        grid=(n_items,),
        in_specs=in_specs,
        out_specs=pl.BlockSpec((n_strips, MOE_ROWS, STRIP),
                               lambda i, ie, ib, lo, hi, nv: (0, blk0 + ib[i], 0)),
        scratch_shapes=[pltpu.VMEM((d, dff2), BF16), pltpu.VMEM((dff, d), BF16),
                        pltpu.VMEM((MOE_ROWS, d // 2), jnp.uint32)],
    )
    return pl.pallas_call(
        _expert_kernel,
        grid_spec=grid_spec,
        out_shape=jax.ShapeDtypeStruct((n_strips, n_rows, STRIP), jnp.uint32),
        input_output_aliases=aliases,
        compiler_params=_params(("arbitrary",), EXPERT_VMEM_LIMIT),
        name="expert_ffn",
    )(*operands)


def _items_for_blocks(items, first_blk, n_blk):
    expert, block, lo, hi, total = items
    valid = jnp.arange(block.shape[0], dtype=jnp.int32) < total
    before = jnp.sum(valid & (block < first_blk)).astype(jnp.int32)
    inside = jnp.sum(valid & (block >= first_blk) & (block < first_blk + n_blk)).astype(jnp.int32)
    steps = jnp.arange(n_blk + N_EXPERTS - 1, dtype=jnp.int32)
    pos = before + jnp.clip(steps, 0, jnp.maximum(inside - 1, 0))
    return expert[pos], block[pos] - first_blk, lo[pos], hi[pos], inside.reshape(1)


def _expert_work_items(counts, n_rows):
    e_ids = jnp.arange(N_EXPERTS, dtype=jnp.int32)
    ends = jnp.cumsum(counts).astype(jnp.int32)
    starts = ends - counts
    first_blk = starts // MOE_ROWS
    last_blk = (ends - 1) // MOE_ROWS
    per_expert = jnp.where(counts > 0, last_blk - first_blk + 1, 0)
    item_end = jnp.cumsum(per_expert).astype(jnp.int32)
    item_start = item_end - per_expert
    total = item_end[-1:]
    max_items = n_rows // MOE_ROWS + N_EXPERTS - 1
    it = jnp.clip(jnp.arange(max_items, dtype=jnp.int32), 0, jnp.maximum(total - 1, 0))
    expert = jnp.sum(item_end[None, :] <= it[:, None], axis=1).astype(jnp.int32)
    pick = lambda v: jnp.sum(jnp.where(expert[:, None] == e_ids, v, 0), axis=1)
    block = pick(first_blk) + it - pick(item_start)
    lo = jnp.maximum(pick(starts) - block * MOE_ROWS, 0)
    hi = jnp.minimum(pick(ends) - block * MOE_ROWS, MOE_ROWS)
    return expert, block, lo, hi, total, starts


def _gather_strips(strips, idx):
    c, r, w = strips.shape
    offsets = jnp.arange(c, dtype=jnp.int32)[:, None] * r
    out = _gather_rows(strips.reshape(c * r, w), (idx[None, :] + offsets).reshape(-1))
    return out.reshape(c, idx.shape[0], w)


def _gather_rows(table, idx):
    m = idx.shape[0]
    w = table.shape[1]
    window = GATHER_WINDOW
    mesh = plsc.VectorSubcoreMesh(core_axis_name="core", subcore_axis_name="subcore")

    @pl.kernel(out_type=jax.ShapeDtypeStruct((m, w), table.dtype), mesh=mesh)
    def gather_kernel(table_hbm, idx_hbm, out_hbm):
        def body(idx_vmem, out_vmem):
            pltpu.sync_copy(table_hbm.at[idx_vmem.at[0]], out_vmem)

        pltpu.emit_pipeline(
            body,
            grid=(m // window,),
            in_specs=[pl.BlockSpec((1, window), index_map=lambda i: (0, i))],
            out_specs=[pl.BlockSpec((window, w), index_map=lambda i: (i, 0))],
            core_axis_name=("core", "subcore"),
            dimension_semantics=(pltpu.PARALLEL,),
        )(idx_hbm, out_hbm)

    return gather_kernel(table, idx.reshape(1, m))


def _combine_kernel(yk_ref, tw_ref, x1_ref, mod_ref, gf_ref, *rest):
    o_ref = rest[-1]
    tw = tw_ref[0]
    lows, highs = [], []
    for c in range(yk_ref.shape[0]):
        lo, hi = _unpack_pairs(yk_ref[c, 0, 0])
        lo, hi = lo * tw[:, 0:1], hi * tw[:, 0:1]
        for k in range(1, TOP_K):
            lo_k, hi_k = _unpack_pairs(yk_ref[c, k, 0])
            lo, hi = lo + lo_k * tw[:, k:k + 1], hi + hi_k * tw[:, k:k + 1]
        lows.append(lo)
        highs.append(hi)
    y = jnp.concatenate(lows + highs, axis=1)
    x2 = x1_ref[0] + mod_ref[0][5:6] * y
    o_ref[0] = _rms(x2, gf_ref[...], 1e-6)


def _combine(yk, tw, x1, mod, g_final, src0, out0, b_total, out_so_far):
    _, s, d = x1.shape
    b = yk.shape[2]
    tm = min(TOKEN_TILE, s)
    gf = g_final.reshape(1, d)
    in_specs = [pl.BlockSpec((d // (2 * STRIP), TOP_K, 1, tm, STRIP), lambda i, j: (0, 0, i, j, 0)),
                pl.BlockSpec((1, tm, TOP_K), lambda i, j: (src0 + i, j, 0)),
                pl.BlockSpec((1, tm, d), lambda i, j: (src0 + i, j, 0)),
                pl.BlockSpec((1, N_MOD, d), lambda i, j: (src0 + i, 0, 0)),
                pl.BlockSpec((1, d), lambda i, j: (0, 0))]
    operands = [yk, tw, x1, mod, gf]
    aliases = {}
    if out_so_far is not None:
        in_specs.append(pl.BlockSpec(memory_space=pl.ANY))
        operands.append(out_so_far)
        aliases = {len(operands) - 1: 0}
    return pl.pallas_call(
        _combine_kernel,
        grid=(b, s // tm),
        in_specs=in_specs,
        out_specs=pl.BlockSpec((1, tm, d), lambda i, j: (out0 + i, j, 0)),
        out_shape=jax.ShapeDtypeStruct((b_total, s, d), F32),
        input_output_aliases=aliases,
        compiler_params=_params(("parallel", "parallel")),
        name="combine_final_norm",
    )(*operands)


def kernel(x, c, positions, w_ada, b_ada, g_norm1, w_in, g_q_norm, w_uq, g_kv_norm, w_ukv,
           lambda_q1, lambda_k1, lambda_q2, lambda_k2, g_subln, w_out, g_norm2,
           w_router, b_router, w_gate_up, b_gate_up, w_down, b_down, g_final):
    b_total, s, d = x.shape
    l = 0
    mod_all = _ada_mod(c, w_ada[l], b_ada[l]).reshape(b_total, N_MOD, d)
    w_in_p, w_dvt, w_uq_p, w_uk_p, w_uvt = _pad_in_weights(w_in[l], w_uq[l], w_ukv[l])
    lam_vecs = jnp.stack([lambda_q1[l], lambda_k1[l], lambda_q2[l], lambda_k2[l]]).astype(F32)
    bgu = (b_gate_up[l].reshape(N_EXPERTS, EXPERT_DFF // LANES, LANES, 2)
           .transpose(0, 1, 3, 2).reshape(N_EXPERTS, 1, 2 * EXPERT_DFF))
    bd = b_down[l].reshape(N_EXPERTS, 1, d)

    groups = BATCH_GROUPS if b_total % BATCH_GROUPS == 0 else 1
    b = b_total // groups
    n = b * s
    out = None
    for g in range(groups):
        b0 = g * b
        mod = mod_all[b0:b0 + b]
        qm, km, vmt, dq, dk, dvt = _in_projection(x, b0, positions[b0:b0 + b], mod, g_norm1[l],
                                                  w_in_p, w_dvt, g_q_norm[l], w_uq_p,
                                                  g_kv_norm[l], w_uk_p, w_uvt)
        o_mla = _mla_attention(qm, km, vmt)
        o_diff = _diff_attention(lam_vecs, g_subln[l], dq, dk, dvt)
        x1, h2, idx, tw, rank, cnt = _out_projection(o_mla, o_diff, x, b0, mod, w_out[l],
                                                     g_norm2[l], w_router[l], b_router[l])

        to_kn = lambda a: a.transpose(1, 0, 2).reshape(TOP_K, n)
        idx, tw, rank = to_kn(idx), to_kn(tw), to_kn(rank)
        *items, starts = _expert_work_items(cnt[:, 0], n * TOP_K)
        e_ids = jnp.arange(N_EXPERTS, dtype=jnp.int32)
        slot = rank + jnp.sum(jnp.where(idx[..., None] == e_ids, starts, 0), axis=-1)
        tok = jnp.arange(n, dtype=jnp.int32)
        row_tok = jnp.sort((idx * n + tok).reshape(-1)) % n

        n_rows = n * TOP_K
        n_blk = n_rows // MOE_ROWS
        row_groups = ROW_GROUPS if n_blk % ROW_GROUPS == 0 else 1
        g_blk = n_blk // row_groups
        h2s = h2.reshape(-1, n, STRIP)
        y = None
        for r in range(row_groups):
            rows = slice(r * g_blk * MOE_ROWS, (r + 1) * g_blk * MOE_ROWS)
            xs = _gather_strips(h2s, row_tok[rows])
            y = _expert_ffn(_items_for_blocks(items, r * g_blk, g_blk), xs, r * g_blk, n_rows, y,
                            w_gate_up[l], bgu, w_down[l], bd)
        token_groups = TOKEN_GROUPS if b % TOKEN_GROUPS == 0 else 1
        nb = b // token_groups
        tw_t = tw.T.reshape(b, s, TOP_K)
        slot_b = slot.reshape(TOP_K, b, s)
        for t0 in range(0, b, nb):
            yk = _gather_strips(y, slot_b[:, t0:t0 + nb].reshape(-1))
            out = _combine(yk.reshape(-1, TOP_K, nb, s, STRIP), tw_t, x1, mod, g_final,
                           t0, b0 + t0, b_total, out)
    return out
```

```python
import functools
import math

import jax
import jax.numpy as jnp
from jax import lax
from jax.experimental import pallas as pl
from jax.experimental.pallas import tpu as pltpu
from jax.experimental.pallas import tpu_sc as plsc

F32 = jnp.float32
BF16 = jnp.bfloat16
HIGHEST = lax.Precision.HIGHEST

D_MODEL = 1024
CHUNK = 64
ROPE_THETA = 500000.0
MLA_HEADS = 8
MLA_NOPE = 64
MLA_ROPE = 32
MLA_V = 64
MLA_QK = MLA_NOPE + MLA_ROPE
Q_RANK = 384
KV_RANK = 256
DIFF_HEADS = 4
DIFF_DIM = 64
DIFF_ROPE = DIFF_DIM // 4
DIFF_WIDTH = DIFF_HEADS * 2 * DIFF_DIM
N_EXPERTS = 32
TOP_K = 4
EXPERT_DFF = D_MODEL
SWIGLU_LIMIT = 7.0
SWIGLU_ALPHA = 1.702
N_MOD = 6
LAMBDA_INIT = 0.8 - 0.6 * math.exp(-0.3 * 0)
LOG2E = math.log2(math.e)

LANES = 128
BF16_SUBLANES = 16
MXU_COLS = 256
HEAD_PAD = LANES
TOKEN_TILE = 512
MLA_TILE = 512
DIFF_TILE = 256
BATCH_GROUPS = 1
ROW_GROUPS = 4
TOKEN_GROUPS = 4
MOE_ROWS = 512
GATHER_WINDOW = 128
STRIP = 256
VMEM_LIMIT = 48 * 1024 * 1024
EXPERT_VMEM_LIMIT = 58 * 1024 * 1024
NEG_BIG = -1e30

MLA_X1 = 96
MLA_HALF = MLA_ROPE // 2
DIFF_HALF = DIFF_ROPE // 2

C_Q = 0
C_KV = Q_RANK
C_KPE = Q_RANK + KV_RANK
C_DQ = C_KPE + LANES
C_DK = C_DQ + DIFF_WIDTH
IN_COLS_PAD = C_DK + DIFF_WIDTH


def _rms(x, g, eps):
    return x * lax.rsqrt(jnp.mean(x * x, axis=-1, keepdims=True) + eps) * g


def _pack_pairs(x):
    w = x.shape[1] // 2
    as_bits = lambda v: lax.bitcast_convert_type(v.astype(BF16).astype(F32), jnp.uint32)
    return (as_bits(x[:, w:]) & jnp.uint32(0xFFFF0000)) | (as_bits(x[:, :w]) >> 16)


def _unpack_pairs(p):
    lo = lax.bitcast_convert_type(p << 16, F32)
    hi = lax.bitcast_convert_type(p & jnp.uint32(0xFFFF0000), F32)
    return lo, hi


def _params(sem, limit=VMEM_LIMIT):
    return pltpu.CompilerParams(dimension_semantics=sem, vmem_limit_bytes=limit)


def _nt(a, b):
    return lax.dot_general(a, b, (((1,), (1,)), ((), ())), preferred_element_type=F32)


def _ada_kernel(c_ref, w_ref, b_ref, o_ref):
    c = c_ref[...]
    sc = c / (1.0 + jnp.exp(-c))
    o_ref[...] = jnp.dot(sc, w_ref[...], preferred_element_type=F32, precision=HIGHEST) + b_ref[...]


def _ada_mod(c, w_ada, b_ada):
    b, d = c.shape
    n = w_ada.shape[1]
    return pl.pallas_call(
        _ada_kernel,
        grid=(n // d,),
        in_specs=[pl.BlockSpec((b, d), lambda j: (0, 0)),
                  pl.BlockSpec((d, d), lambda j: (0, j)),
                  pl.BlockSpec((1, d), lambda j: (0, j))],
        out_specs=pl.BlockSpec((b, d), lambda j: (0, j)),
        out_shape=jax.ShapeDtypeStruct((b, n), F32),
        compiler_params=_params(("arbitrary",)),
        name="ada_mod",
    )(c, w_ada, b_ada.reshape(1, n))


def _rope(t, cos, sin_a, sin_b, shift):
    return (t * cos + pltpu.roll(t, LANES - shift, 1) * sin_a + pltpu.roll(t, shift, 1) * sin_b)


def _inproj_kernel(x_ref, pos_ref, mod_ref, g1_ref, win_ref, wdvt_ref, gq_ref, wuq_ref, gkv_ref,
                   wuk_ref, wuvt_ref, freq_ref, mask_ref,
                   qm_ref, km_ref, vmt_ref, dq_ref, dk_ref, dvt_ref):
    x = x_ref[0]
    mod = mod_ref[0]
    h = (_rms(x, g1_ref[...], 1e-6) * (1.0 + mod[1:2]) + mod[0:1]).astype(BF16)
    proj = jnp.dot(h, win_ref[...], preferred_element_type=F32)
    dvt_ref[0] = _nt(wdvt_ref[...], h).astype(BF16)

    ang = pos_ref[0].astype(F32) * freq_ref[...]
    cos, sin = jnp.cos(ang), jnp.sin(ang)
    mk = mask_ref[...]
    cos_m = jnp.where(mk[0:1] > 0, cos, 1.0)
    sa_m, sb_m = sin * mk[1:2], sin * mk[2:3]
    cos_d = jnp.where(mk[3:4] > 0, cos, 1.0)
    sa_d, sb_d = sin * mk[4:5], sin * mk[5:6]
    mla_scale = MLA_QK ** -0.5 * LOG2E
    diff_scale = DIFF_DIM ** -0.5 * LOG2E

    cq = _rms(proj[:, C_Q:C_Q + Q_RANK], gq_ref[...], 1e-6)
    q = jnp.dot(cq.astype(BF16), wuq_ref[...], preferred_element_type=F32)
    ckv = _rms(proj[:, C_KV:C_KV + KV_RANK], gkv_ref[...], 1e-6).astype(BF16)
    kn = jnp.dot(ckv, wuk_ref[...], preferred_element_type=F32)
    vmt_ref[0] = _nt(wuvt_ref[...], ckv).astype(BF16)
    kpe = _rope(proj[:, C_KPE:C_KPE + LANES], cos_m, sa_m, sb_m, MLA_HALF)
    cq_m, saq_m, sbq_m = cos_m * mla_scale, sa_m * mla_scale, sb_m * mla_scale
    for hd in range(MLA_HEADS):
        sl = slice(hd * HEAD_PAD, (hd + 1) * HEAD_PAD)
        qm_ref[0, :, sl] = _rope(q[:, sl], cq_m, saq_m, sbq_m, MLA_HALF).astype(BF16)
        km_ref[0, :, sl] = (kn[:, sl] + kpe).astype(BF16)

    cq_d, saq_d, sbq_d = cos_d * diff_scale, sa_d * diff_scale, sb_d * diff_scale
    for hd in range(DIFF_HEADS):
        sl = slice(hd * LANES, (hd + 1) * LANES)
        tq = proj[:, C_DQ + hd * LANES:C_DQ + (hd + 1) * LANES]
        tk = proj[:, C_DK + hd * LANES:C_DK + (hd + 1) * LANES]
        dq_ref[0, :, sl] = _rope(tq, cq_d, saq_d, sbq_d, DIFF_HALF).astype(BF16)
        dk_ref[0, :, sl] = _rope(tk, cos_d, sa_d, sb_d, DIFF_HALF).astype(BF16)


def _rope_tables():
    lane = jnp.arange(LANES)
    f_m = ROPE_THETA ** (-jnp.arange(MLA_HALF, dtype=F32) / MLA_HALF)
    f_d = ROPE_THETA ** (-jnp.arange(DIFF_HALF, dtype=F32) / DIFF_HALF)
    m_x1 = (lane >= MLA_X1) & (lane < MLA_X1 + MLA_HALF)
    m_x2 = (lane >= MLA_X1 + MLA_HALF) & (lane < MLA_X1 + 2 * MLA_HALF)
    d_x1 = (lane % DIFF_DIM) < DIFF_HALF
    d_x2 = ((lane % DIFF_DIM) >= DIFF_HALF) & ((lane % DIFF_DIM) < 2 * DIFF_HALF)
    freq = jnp.where(m_x1 | m_x2, f_m[(lane - MLA_X1) % MLA_HALF], 0.0)
    freq = jnp.where(d_x1 | d_x2, f_d[lane % DIFF_HALF], freq)
    z = jnp.zeros((LANES,), F32)
    masks = jnp.stack([(m_x1 | m_x2).astype(F32), -m_x1.astype(F32), m_x2.astype(F32),
                       (d_x1 | d_x2).astype(F32), -d_x1.astype(F32), d_x2.astype(F32), z, z])
    return freq.reshape(1, LANES).astype(F32), masks


def _pad_in_weights(w_in, w_uq, w_ukv):
    d = w_in.shape[0]
    c_kpe = Q_RANK + KV_RANK
    c_dv = c_kpe + MLA_ROPE + 2 * DIFF_WIDTH
    kpe = jnp.concatenate([jnp.zeros((d, MLA_X1), F32), w_in[:, c_kpe:c_kpe + MLA_ROPE]], axis=1)
    w_in_p = jnp.concatenate([w_in[:, :c_kpe], kpe, w_in[:, c_kpe + MLA_ROPE:c_dv]], axis=1)
    w_dvt = w_in[:, c_dv:].T
    uq = w_uq.reshape(Q_RANK, MLA_HEADS, MLA_QK)
    uq_p = jnp.concatenate([uq[:, :, :MLA_NOPE],
                            jnp.zeros((Q_RANK, MLA_HEADS, MLA_X1 - MLA_NOPE), F32),
                            uq[:, :, MLA_NOPE:]], axis=2).reshape(Q_RANK, MLA_HEADS * HEAD_PAD)
    ukv = w_ukv.reshape(KV_RANK, MLA_HEADS, MLA_NOPE + MLA_V)
    uk_p = jnp.concatenate([ukv[:, :, :MLA_NOPE],
                            jnp.zeros((KV_RANK, MLA_HEADS, HEAD_PAD - MLA_NOPE), F32)],
                           axis=2).reshape(KV_RANK, MLA_HEADS * HEAD_PAD)
    uvt = ukv[:, :, MLA_NOPE:].reshape(KV_RANK, MLA_HEADS * MLA_V).T
    return (w_in_p.astype(BF16), w_dvt.astype(BF16), uq_p.astype(BF16), uk_p.astype(BF16),
            uvt.astype(BF16))


def _in_projection(x, b0, positions, mod, g_norm1, w_in_p, w_dvt, g_q, w_uq_p, g_kv, w_uk_p, w_uvt):
    _, s, d = x.shape
    b = positions.shape[0]
    tm = min(TOKEN_TILE, s)
    freq, masks = _rope_tables()
    full = lambda a: pl.BlockSpec(a.shape, lambda i, j: (0,) * a.ndim)
    tile = lambda w: pl.BlockSpec((1, tm, w), lambda i, j: (i, j, 0))
    x_tile = pl.BlockSpec((1, tm, d), lambda i, j: (b0 + i, j, 0))
    tile_t = lambda w: pl.BlockSpec((1, w, tm), lambda i, j: (i, 0, j))
    wide, vm_w = MLA_HEADS * HEAD_PAD, MLA_HEADS * MLA_V
    g1, gq, gkv = g_norm1.reshape(1, d), g_q.reshape(1, Q_RANK), g_kv.reshape(1, KV_RANK)
    tok = lambda w: jax.ShapeDtypeStruct((b, s, w), BF16)
    feat = lambda w: jax.ShapeDtypeStruct((b, w, s), BF16)
    return pl.pallas_call(
        _inproj_kernel,
        grid=(b, s // tm),
        in_specs=[x_tile, tile(1), pl.BlockSpec((1, N_MOD, d), lambda i, j: (i, 0, 0)),
                  full(g1), full(w_in_p), full(w_dvt), full(gq), full(w_uq_p), full(gkv),
                  full(w_uk_p), full(w_uvt), full(freq), full(masks)],
        out_specs=[tile(wide), tile(wide), tile_t(vm_w), tile(DIFF_WIDTH), tile(DIFF_WIDTH),
                   tile_t(DIFF_WIDTH)],
        out_shape=[tok(wide), tok(wide), feat(vm_w), tok(DIFF_WIDTH), tok(DIFF_WIDTH),
                   feat(DIFF_WIDTH)],
        compiler_params=_params(("parallel", "parallel")),
        name="in_projection",
    )(x, positions.reshape(b, s, 1), mod, g1, w_in_p, w_dvt, gq, w_uq_p, gkv, w_uk_p, w_uvt,
      freq, masks)


def _diag_mask(t):
    key = lax.broadcasted_iota(jnp.int32, (t, t), 0) // CHUNK
    qry = lax.broadcasted_iota(jnp.int32, (t, t), 1) // CHUNK
    return key <= qry


def _scores_t(q, k_ref, lanes, q0, t):
    diag = jnp.where(_diag_mask(t), _nt(k_ref[0, q0:q0 + t, lanes], q), NEG_BIG)
    bulk = _nt(k_ref[0, 0:q0, lanes], q) if q0 else None
    return bulk, diag


def _softmax_sums_t(bulk, diag):
    m = jnp.max(diag, axis=0, keepdims=True)
    if bulk is not None:
        m = jnp.maximum(m, jnp.max(bulk, axis=0, keepdims=True))
    p_diag = jnp.exp2(diag - m)
    l = jnp.sum(p_diag, axis=0, keepdims=True)
    p_bulk = None
    if bulk is not None:
        p_bulk = jnp.exp2(bulk - m)
        l = l + jnp.sum(p_bulk, axis=0, keepdims=True)
    return p_bulk, p_diag, l


def _pv_plain_t(vt_ref, rows, p_bulk, p_diag, q0, t):
    o = jnp.dot(vt_ref[0, rows, q0:q0 + t], p_diag.astype(BF16), preferred_element_type=F32)
    if p_bulk is not None:
        o = o + jnp.dot(vt_ref[0, rows, 0:q0], p_bulk.astype(BF16), preferred_element_type=F32)
    return o


def _softmax_t(bulk, diag):
    m = jnp.max(diag, axis=0, keepdims=True)
    if bulk is not None:
        m = jnp.maximum(m, jnp.max(bulk, axis=0, keepdims=True))
    p_diag = jnp.exp2(diag - m).astype(BF16)
    p_bulk = jnp.exp2(bulk - m).astype(BF16) if bulk is not None else None
    return p_bulk, p_diag


def _pv_t(vt_ref, rows, p_bulk, p_diag, q0, t):
    n = rows.stop - rows.start

    def with_ones(v):
        return jnp.concatenate([v, jnp.ones((BF16_SUBLANES, v.shape[1]), BF16)], axis=0)

    o = jnp.dot(with_ones(vt_ref[0, rows, q0:q0 + t]), p_diag, preferred_element_type=F32)
    if p_bulk is not None:
        o = o + jnp.dot(with_ones(vt_ref[0, rows, 0:q0]), p_bulk, preferred_element_type=F32)
    return o[:n], o[n:n + 1]


def _mla_attn_kernel(q_ref, k_ref, vt_ref, o_ref, *, seq, t):
    units = [(q0, i) for q0 in range(0, seq, t) for i in range(2)]

    def scores(q0, i):
        lanes = slice(i * HEAD_PAD, (i + 1) * HEAD_PAD)
        return _scores_t(q_ref[0, q0:q0 + t, lanes], k_ref, lanes, q0, t)

    ahead = scores(*units[0])
    outs = []
    for u, (q0, i) in enumerate(units):
        bulk, diag = ahead
        ahead = scores(*units[u + 1]) if u + 1 < len(units) else None
        p_bulk, p_diag, l = _softmax_sums_t(bulk, diag)
        rows = slice(i * MLA_V, (i + 1) * MLA_V)
        outs.append(_pv_plain_t(vt_ref, rows, p_bulk, p_diag, q0, t) / l)
        if i == 1:
            o_ref[0, q0:q0 + t, :] = jnp.concatenate(outs, axis=0).T.astype(BF16)
            outs = []


def _diff_attn_kernel(lam_ref, gs_ref, q_ref, k_ref, vt_ref, o_ref, *, seq, t):
    lv = lam_ref[...]
    lam = (jnp.exp(jnp.sum(lv[0:1] * lv[1:2], axis=-1, keepdims=True))
           - jnp.exp(jnp.sum(lv[2:3] * lv[3:4], axis=-1, keepdims=True)) + LAMBDA_INIT)
    lane = lax.broadcasted_iota(jnp.int32, (t, 2 * DIFF_DIM), 1)
    every = slice(0, 2 * DIFF_DIM)
    def scores(q0):
        q = q_ref[0, q0:q0 + t, :]
        zero = jnp.zeros_like(q)
        return (_scores_t(jnp.where(lane < DIFF_DIM, q, zero), k_ref, every, q0, t),
                _scores_t(jnp.where(lane >= DIFF_DIM, q, zero), k_ref, every, q0, t))

    ahead = scores(0)
    for q0 in range(0, seq, t):
        (b1, d1), (b2, d2) = ahead
        ahead = scores(q0 + t) if q0 + t < seq else None
        o1, l1 = _pv_t(vt_ref, every, *_softmax_t(b1, d1), q0, t)
        o2, l2 = _pv_t(vt_ref, every, *_softmax_t(b2, d2), q0, t)
        o = o1 / l1 - lam * (o2 / l2)
        o = (o * lax.rsqrt(jnp.mean(o * o, axis=0, keepdims=True) + 1e-5) * gs_ref[...]
             * (1.0 - LAMBDA_INIT))
        o_ref[0, q0:q0 + t, :] = o.T.astype(BF16)


def _mla_attention(qm, km, vmt):
    b, s, _ = qm.shape
    t = min(MLA_TILE, s)
    pair = 2 * HEAD_PAD
    return pl.pallas_call(
        functools.partial(_mla_attn_kernel, seq=s, t=t),
        grid=(b, MLA_HEADS // 2),
        in_specs=[pl.BlockSpec((1, s, pair), lambda i, j: (i, 0, j)),
                  pl.BlockSpec((1, s, pair), lambda i, j: (i, 0, j)),
                  pl.BlockSpec((1, LANES, s), lambda i, j: (i, j, 0))],
        out_specs=pl.BlockSpec((1, s, LANES), lambda i, j: (i, 0, j)),
        out_shape=jax.ShapeDtypeStruct((b, s, MLA_HEADS * MLA_V), BF16),
        compiler_params=_params(("parallel", "parallel")),
        name="mla_attention",
    )(qm, km, vmt)


def _diff_attention(lam_vecs, g_subln, dq, dk, dvt):
    b, s, _ = dq.shape
    t = min(DIFF_TILE, s)
    blk = pl.BlockSpec((1, s, LANES), lambda i, j: (i, 0, j))
    gs = g_subln.reshape(2 * DIFF_DIM, 1)
    return pl.pallas_call(
        functools.partial(_diff_attn_kernel, seq=s, t=t),
        grid=(b, DIFF_HEADS),
        in_specs=[pl.BlockSpec(lam_vecs.shape, lambda i, j: (0, 0)),
                  pl.BlockSpec(gs.shape, lambda i, j: (0, 0)), blk, blk,
                  pl.BlockSpec((1, LANES, s), lambda i, j: (i, j, 0))],
        out_specs=blk,
        out_shape=jax.ShapeDtypeStruct((b, s, DIFF_WIDTH), BF16),
        compiler_params=_params(("parallel", "parallel")),
        name="diff_attention",
    )(lam_vecs, gs, dq, dk, dvt)


def _outproj_kernel(om_ref, od_ref, x_ref, mod_ref, wout_ref, g2_ref, wr_ref, br_ref,
                    x1_ref, h2_ref, idx_ref, tw_ref, rank_ref, cnt_ref, carry_ref):
    first = (pl.program_id(0) == 0) & (pl.program_id(1) == 0)

    @pl.when(first)
    def _():
        carry_ref[...] = jnp.zeros(carry_ref.shape, F32)

    half = om_ref.shape[-1]
    mix = (jnp.dot(om_ref[0], wout_ref[:half, :], preferred_element_type=F32)
           + jnp.dot(od_ref[0], wout_ref[half:, :], preferred_element_type=F32))
    mod = mod_ref[0]
    x1 = x_ref[0] + mod[2:3] * mix
    h2 = _rms(x1, g2_ref[...], 1e-6) * (1.0 + mod[4:5]) + mod[3:4]
    x1_ref[0] = x1
    packed = _pack_pairs(h2)
    for c in range(h2_ref.shape[0]):
        h2_ref[c, 0] = packed[:, c * STRIP:(c + 1) * STRIP]

    logits = lax.dot_general(wr_ref[...], h2, (((1,), (1,)), ((), ())),
                             preferred_element_type=F32, precision=HIGHEST) + br_ref[...]
    n_e, t = logits.shape
    e_iota = lax.broadcasted_iota(jnp.int32, (n_e, t), 0)
    vals, idxs = [], []
    rest = logits
    for _ in range(TOP_K):
        m = jnp.max(rest, axis=0, keepdims=True)
        ik = jnp.min(jnp.where(rest == m, e_iota, n_e), axis=0, keepdims=True)
        vals.append(m)
        idxs.append(ik)
        rest = jnp.where(e_iota == ik, -jnp.inf, rest)
    ex = [jnp.exp(v - vals[0]) for v in vals]
    den = ex[0] + ex[1] + ex[2] + ex[3]
    sel = (e_iota == idxs[0]) | (e_iota == idxs[1]) | (e_iota == idxs[2]) | (e_iota == idxs[3])

    before = (lax.broadcasted_iota(jnp.int32, (t, t), 0)
              < lax.broadcasted_iota(jnp.int32, (t, t), 1)).astype(BF16)
    prefix = jnp.dot(sel.astype(BF16), before, preferred_element_type=F32)
    pos = carry_ref[...] + prefix
    for k in range(TOP_K):
        idx_ref[0, k:k + 1, :] = idxs[k]
        tw_ref[0, k:k + 1, :] = ex[k] / den
        rk = jnp.sum(jnp.where(e_iota == idxs[k], pos, 0.0), axis=0, keepdims=True)
        rank_ref[0, k:k + 1, :] = rk.astype(jnp.int32)
    total = carry_ref[...] + jnp.sum(sel.astype(F32), axis=1, keepdims=True)
    carry_ref[...] = total
    cnt_ref[...] = total.astype(jnp.int32)


def _out_projection(o_mla, o_diff, x, b0, mod, w_out, g_norm2, w_router, b_router):
    _, s, d = x.shape
    b = o_mla.shape[0]
    tm = min(TOKEN_TILE, s)
    nt = s // tm
    half = o_mla.shape[-1]
    full = lambda a: pl.BlockSpec(a.shape, lambda i, j: (0,) * a.ndim)
    tile = lambda w: pl.BlockSpec((1, tm, w), lambda i, j: (i, j, 0))
    x_tile = pl.BlockSpec((1, tm, d), lambda i, j: (b0 + i, j, 0))
    route = pl.BlockSpec((1, TOP_K, tm), lambda i, j: (i * nt + j, 0, 0))
    g2 = g_norm2.reshape(1, d)
    wr = w_router.T
    br = b_router.reshape(N_EXPERTS, 1)
    w_out_b = w_out.astype(BF16)
    n_strips = d // (2 * STRIP)
    return pl.pallas_call(
        _outproj_kernel,
        grid=(b, nt),
        in_specs=[tile(half), tile(half), x_tile, pl.BlockSpec((1, N_MOD, d), lambda i, j: (i, 0, 0)),
                  full(w_out_b), full(g2), full(wr), full(br)],
        out_specs=[tile(d), pl.BlockSpec((n_strips, 1, tm, STRIP), lambda i, j: (0, i, j, 0)),
                   route, route, route,
                   pl.BlockSpec((N_EXPERTS, 1), lambda i, j: (0, 0))],
        out_shape=[jax.ShapeDtypeStruct((b, s, d), F32),
                   jax.ShapeDtypeStruct((n_strips, b, s, STRIP), jnp.uint32),
                   jax.ShapeDtypeStruct((b * nt, TOP_K, tm), jnp.int32),
                   jax.ShapeDtypeStruct((b * nt, TOP_K, tm), F32),
                   jax.ShapeDtypeStruct((b * nt, TOP_K, tm), jnp.int32),
                   jax.ShapeDtypeStruct((N_EXPERTS, 1), jnp.int32)],
        scratch_shapes=[pltpu.VMEM((N_EXPERTS, 1), F32)],
        compiler_params=_params(("arbitrary", "arbitrary")),
        name="out_projection_router",
    )(o_mla, o_diff, x, mod, w_out_b, g2, wr, br)


def _split_gate_up_perm():
    r = lax.broadcasted_iota(jnp.int32, (MXU_COLS, MXU_COLS), 0)
    c = lax.broadcasted_iota(jnp.int32, (MXU_COLS, MXU_COLS), 1)
    src = jnp.where(c < LANES, 2 * c, 2 * (c - LANES) + 1)
    return (r == src).astype(BF16)


def _expert_kernel(ie_ref, ib_ref, lo_ref, hi_ref, nv_ref, xs_ref, wgu_ref, bgu_ref, wd_ref, bd_ref,
                   *rest):
    y_ref, wgu_s, wd_s, y_s = rest[-4:]
    i = pl.program_id(0)
    prev = jnp.maximum(i - 1, 0)
    valid = i < nv_ref[0]
    new_expert = (i == 0) | (ie_ref[i] != ie_ref[prev])
    new_block = (i == 0) | (ib_ref[i] != ib_ref[prev])
    n_groups = wgu_s.shape[1] // MXU_COLS
    n_strips = y_ref.shape[0]
    row = lax.broadcasted_iota(jnp.int32, (y_ref.shape[1], 1), 0)
    mine = (row >= lo_ref[i]) & (row < hi_ref[i])

    @pl.when(valid & new_expert)
    def _():
        perm = _split_gate_up_perm()
        for c in range(n_groups):
            cols = slice(c * MXU_COLS, (c + 1) * MXU_COLS)
            w = wgu_ref[0, :, cols].astype(BF16)
            wgu_s[:, cols] = jnp.dot(w, perm, preferred_element_type=F32).astype(BF16)
        wd_s[...] = wd_ref[0].astype(BF16)

    @pl.when(valid)
    def _():
        lo, hi = _unpack_pairs(jnp.concatenate([xs_ref[c] for c in range(n_strips)], axis=1))
        xb = jnp.concatenate([lo, hi], axis=1).astype(BF16)
        gu = jnp.dot(xb, wgu_s[...], preferred_element_type=F32) + bgu_ref[0]
        acts = []
        for c in range(n_groups):
            g = jnp.minimum(gu[:, c * MXU_COLS:c * MXU_COLS + LANES], SWIGLU_LIMIT)
            u = jnp.clip(gu[:, c * MXU_COLS + LANES:(c + 1) * MXU_COLS], -SWIGLU_LIMIT, SWIGLU_LIMIT)
            acts.append((g / (1.0 + jnp.exp(-SWIGLU_ALPHA * g)) * (u + 1.0)).astype(BF16))
        act = jnp.concatenate(acts, axis=1)
        y_s[...] = _pack_pairs(jnp.dot(act, wd_s[...], preferred_element_type=F32) + bd_ref[0])

    @pl.when(valid & new_block)
    def _():
        for c in range(n_strips):
            y_ref[c] = jnp.where(mine, y_s[:, c * STRIP:(c + 1) * STRIP], jnp.uint32(0))

    @pl.when(valid & jnp.logical_not(new_block))
    def _():
        for c in range(n_strips):
            y_ref[c] = jnp.where(mine, y_s[:, c * STRIP:(c + 1) * STRIP], y_ref[c])


def _expert_ffn(items, xs, blk0, n_rows, y_so_far, w_gate_up, b_gate_up_grouped, w_down, b_down):
    n_strips = xs.shape[0]
    d = 2 * n_strips * STRIP
    dff2 = w_gate_up.shape[-1]
    dff = dff2 // 2
    n_items = items[0].shape[0]
    wmap = lambda i, ie, ib, lo, hi, nv: (ie[i], 0, 0)
    in_specs = [pl.BlockSpec((n_strips, MOE_ROWS, STRIP), lambda i, ie, ib, lo, hi, nv: (0, ib[i], 0)),
                pl.BlockSpec((1, d, dff2), wmap), pl.BlockSpec((1, 1, dff2), wmap),
                pl.BlockSpec((1, dff, d), wmap), pl.BlockSpec((1, 1, d), wmap)]
    operands = [*items, xs, w_gate_up, b_gate_up_grouped, w_down, b_down]
    aliases = {}
    if y_so_far is not None:
        in_specs.append(pl.BlockSpec(memory_space=pl.ANY))
        operands.append(y_so_far)
        aliases = {len(operands) - 1: 0}
    grid_spec = pltpu.PrefetchScalarGridSpec(
        num_scalar_prefetch=5,
        grid=(n_items,),
        in_specs=in_specs,
        out_specs=pl.BlockSpec((n_strips, MOE_ROWS, STRIP),
                               lambda i, ie, ib, lo, hi, nv: (0, blk0 + ib[i], 0)),
        scratch_shapes=[pltpu.VMEM((d, dff2), BF16), pltpu.VMEM((dff, d), BF16),
                        pltpu.VMEM((MOE_ROWS, d // 2), jnp.uint32)],
    )
    return pl.pallas_call(
        _expert_kernel,
        grid_spec=grid_spec,
        out_shape=jax.ShapeDtypeStruct((n_strips, n_rows, STRIP), jnp.uint32),
        input_output_aliases=aliases,
        compiler_params=_params(("arbitrary",), EXPERT_VMEM_LIMIT),
        name="expert_ffn",
    )(*operands)


def _items_for_blocks(items, first_blk, n_blk):
    expert, block, lo, hi, total = items
    valid = jnp.arange(block.shape[0], dtype=jnp.int32) < total
    before = jnp.sum(valid & (block < first_blk)).astype(jnp.int32)
    inside = jnp.sum(valid & (block >= first_blk) & (block < first_blk + n_blk)).astype(jnp.int32)
    steps = jnp.arange(n_blk + N_EXPERTS - 1, dtype=jnp.int32)
    pos = before + jnp.clip(steps, 0, jnp.maximum(inside - 1, 0))
    return expert[pos], block[pos] - first_blk, lo[pos], hi[pos], inside.reshape(1)


def _expert_work_items(counts, n_rows):
    e_ids = jnp.arange(N_EXPERTS, dtype=jnp.int32)
    ends = jnp.cumsum(counts).astype(jnp.int32)
    starts = ends - counts
    first_blk = starts // MOE_ROWS
    last_blk = (ends - 1) // MOE_ROWS
    per_expert = jnp.where(counts > 0, last_blk - first_blk + 1, 0)
    item_end = jnp.cumsum(per_expert).astype(jnp.int32)
    item_start = item_end - per_expert
    total = item_end[-1:]
    max_items = n_rows // MOE_ROWS + N_EXPERTS - 1
    it = jnp.clip(jnp.arange(max_items, dtype=jnp.int32), 0, jnp.maximum(total - 1, 0))
    expert = jnp.sum(item_end[None, :] <= it[:, None], axis=1).astype(jnp.int32)
    pick = lambda v: jnp.sum(jnp.where(expert[:, None] == e_ids, v, 0), axis=1)
    block = pick(first_blk) + it - pick(item_start)
    lo = jnp.maximum(pick(starts) - block * MOE_ROWS, 0)
    hi = jnp.minimum(pick(ends) - block * MOE_ROWS, MOE_ROWS)
    return expert, block, lo, hi, total, starts


def _gather_strips(strips, idx):
    c, r, w = strips.shape
    offsets = jnp.arange(c, dtype=jnp.int32)[:, None] * r
    out = _gather_rows(strips.reshape(c * r, w), (idx[None, :] + offsets).reshape(-1))
    return out.reshape(c, idx.shape[0], w)


def _gather_rows(table, idx):
    m = idx.shape[0]
    w = table.shape[1]
    window = GATHER_WINDOW
    mesh = plsc.VectorSubcoreMesh(core_axis_name="core", subcore_axis_name="subcore")

    @pl.kernel(out_type=jax.ShapeDtypeStruct((m, w), table.dtype), mesh=mesh)
    def gather_kernel(table_hbm, idx_hbm, out_hbm):
        def body(idx_vmem, out_vmem):
            pltpu.sync_copy(table_hbm.at[idx_vmem.at[0]], out_vmem)

        pltpu.emit_pipeline(
            body,
            grid=(m // window,),
            in_specs=[pl.BlockSpec((1, window), index_map=lambda i: (0, i))],
            out_specs=[pl.BlockSpec((window, w), index_map=lambda i: (i, 0))],
            core_axis_name=("core", "subcore"),
            dimension_semantics=(pltpu.PARALLEL,),
        )(idx_hbm, out_hbm)

    return gather_kernel(table, idx.reshape(1, m))


def _combine_kernel(yk_ref, tw_ref, x1_ref, mod_ref, gf_ref, *rest):
    o_ref = rest[-1]
    tw = tw_ref[0]
    lows, highs = [], []
    for c in range(yk_ref.shape[0]):
        lo, hi = _unpack_pairs(yk_ref[c, 0, 0])
        lo, hi = lo * tw[:, 0:1], hi * tw[:, 0:1]
        for k in range(1, TOP_K):
            lo_k, hi_k = _unpack_pairs(yk_ref[c, k, 0])
            lo, hi = lo + lo_k * tw[:, k:k + 1], hi + hi_k * tw[:, k:k + 1]
        lows.append(lo)
        highs.append(hi)
    y = jnp.concatenate(lows + highs, axis=1)
    x2 = x1_ref[0] + mod_ref[0][5:6] * y
    o_ref[0] = _rms(x2, gf_ref[...], 1e-6)


def _combine(yk, tw, x1, mod, g_final, src0, out0, b_total, out_so_far):
    _, s, d = x1.shape
    b = yk.shape[2]
    tm = min(TOKEN_TILE, s)
    gf = g_final.reshape(1, d)
    in_specs = [pl.BlockSpec((d // (2 * STRIP), TOP_K, 1, tm, STRIP), lambda i, j: (0, 0, i, j, 0)),
                pl.BlockSpec((1, tm, TOP_K), lambda i, j: (src0 + i, j, 0)),
                pl.BlockSpec((1, tm, d), lambda i, j: (src0 + i, j, 0)),
                pl.BlockSpec((1, N_MOD, d), lambda i, j: (src0 + i, 0, 0)),
                pl.BlockSpec((1, d), lambda i, j: (0, 0))]
    operands = [yk, tw, x1, mod, gf]
    aliases = {}
    if out_so_far is not None:
        in_specs.append(pl.BlockSpec(memory_space=pl.ANY))
        operands.append(out_so_far)
        aliases = {len(operands) - 1: 0}
    return pl.pallas_call(
        _combine_kernel,
        grid=(b, s // tm),
        in_specs=in_specs,
        out_specs=pl.BlockSpec((1, tm, d), lambda i, j: (out0 + i, j, 0)),
        out_shape=jax.ShapeDtypeStruct((b_total, s, d), F32),
        input_output_aliases=aliases,
        compiler_params=_params(("parallel", "parallel")),
        name="combine_final_norm",
    )(*operands)


def kernel(x, c, positions, w_ada, b_ada, g_norm1, w_in, g_q_norm, w_uq, g_kv_norm, w_ukv,
           lambda_q1, lambda_k1, lambda_q2, lambda_k2, g_subln, w_out, g_norm2,
           w_router, b_router, w_gate_up, b_gate_up, w_down, b_down, g_final):
    b_total, s, d = x.shape
    l = 0
    mod_all = _ada_mod(c, w_ada[l], b_ada[l]).reshape(b_total, N_MOD, d)
    w_in_p, w_dvt, w_uq_p, w_uk_p, w_uvt = _pad_in_weights(w_in[l], w_uq[l], w_ukv[l])
    lam_vecs = jnp.stack([lambda_q1[l], lambda_k1[l], lambda_q2[l], lambda_k2[l]]).astype(F32)
    bgu = (b_gate_up[l].reshape(N_EXPERTS, EXPERT_DFF // LANES, LANES, 2)
           .transpose(0, 1, 3, 2).reshape(N_EXPERTS, 1, 2 * EXPERT_DFF))
    bd = b_down[l].reshape(N_EXPERTS, 1, d)

    groups = BATCH_GROUPS if b_total % BATCH_GROUPS == 0 else 1
    b = b_total // groups
    n = b * s
    out = None
    for g in range(groups):
        b0 = g * b
        mod = mod_all[b0:b0 + b]
        qm, km, vmt, dq, dk, dvt = _in_projection(x, b0, positions[b0:b0 + b], mod, g_norm1[l],
                                                  w_in_p, w_dvt, g_q_norm[l], w_uq_p,
                                                  g_kv_norm[l], w_uk_p, w_uvt)
        o_mla = _mla_attention(qm, km, vmt)
        o_diff = _diff_attention(lam_vecs, g_subln[l], dq, dk, dvt)
        x1, h2, idx, tw, rank, cnt = _out_projection(o_mla, o_diff, x, b0, mod, w_out[l],
                                                     g_norm2[l], w_router[l], b_router[l])

        to_kn = lambda a: a.transpose(1, 0, 2).reshape(TOP_K, n)
        idx, tw, rank = to_kn(idx), to_kn(tw), to_kn(rank)
        *items, starts = _expert_work_items(cnt[:, 0], n * TOP_K)
        e_ids = jnp.arange(N_EXPERTS, dtype=jnp.int32)
        slot = rank + jnp.sum(jnp.where(idx[..., None] == e_ids, starts, 0), axis=-1)
        tok = jnp.arange(n, dtype=jnp.int32)
        row_tok = jnp.sort((idx * n + tok).reshape(-1)) % n

        n_rows = n * TOP_K
        n_blk = n_rows // MOE_ROWS
        row_groups = ROW_GROUPS if n_blk % ROW_GROUPS == 0 else 1
        g_blk = n_blk // row_groups
        h2s = h2.reshape(-1, n, STRIP)
        y = None
        for r in range(row_groups):
            rows = slice(r * g_blk * MOE_ROWS, (r + 1) * g_blk * MOE_ROWS)
            xs = _gather_strips(h2s, row_tok[rows])
            y = _expert_ffn(_items_for_blocks(items, r * g_blk, g_blk), xs, r * g_blk, n_rows, y,
                            w_gate_up[l], bgu, w_down[l], bd)
        token_groups = TOKEN_GROUPS if b % TOKEN_GROUPS == 0 else 1
        nb = b // token_groups
        tw_t = tw.T.reshape(b, s, TOP_K)
        slot_b = slot.reshape(TOP_K, b, s)
        for t0 in range(0, b, nb):
            yk = _gather_strips(y, slot_b[:, t0:t0 + nb].reshape(-1))
            out = _combine(yk.reshape(-1, TOP_K, nb, s, STRIP), tw_t, x1, mod, g_final,
                           t0, b0 + t0, b_total, out)
    return out
```
